```python
import jax, jax.numpy as jnp
from jax import lax
import numpy as np

D_MODEL = 4096
BATCH = 4
SEQ = 2048
DEPTH = 1
DEC_BATCH = 32
DEC_SEQ = 64
PAST_LEN = 1024

CHUNK = 64
N_HEADS = 16
HEAD_DIM = 128
W_ATT = N_HEADS * HEAD_DIM
W_CONV = D_MODEL // 2
CONV_WIDTH = 3
Q_BLOCK = 128
DN_ALPHA = (2.0 * DEPTH) ** 0.25
DN_BETA = (8.0 * DEPTH) ** -0.25
LN_EPS = 1e-5

kernel_name = 'stickbreak_shortconv_gated_hybrid_step'


def _split_sizes():
    sizes = [W_ATT] * 4 + [W_CONV] * 4 + [D_MODEL] * 2
    return [int(o) for o in np.cumsum(sizes)[:-1]]


def _layer_norm(x, g, b):
    xf = x.astype(jnp.float32)
    mu = jnp.mean(xf, axis=-1, keepdims=True)
    var = jnp.mean(jnp.square(xf - mu), axis=-1, keepdims=True)
    y = (xf - mu) * lax.rsqrt(var + LN_EPS) * g.astype(jnp.float32) + b.astype(jnp.float32)
    return y.astype(x.dtype)


def _stick_breaking(q, k, v, q_pos, k_pos):
    z = jnp.einsum('bqhd,bkhd->bhqk', q.astype(jnp.float32), k.astype(jnp.float32)) * (HEAD_DIM ** -0.5)
    causal = k_pos[None, :] < q_pos[:, None]
    log_keep = jnp.where(causal, jax.nn.log_sigmoid(-z), 0.0)
    after = lax.cumsum(log_keep, axis=3, reverse=True) - log_keep
    w = jnp.where(causal, jnp.exp(jax.nn.log_sigmoid(z) + after), 0.0)
    o = jnp.einsum('bhqk,bkhd->bqhd', w, v.astype(jnp.float32))
    return o.astype(v.dtype)


def _layer(x, past_k, past_v, past_conv, w_in, b_in, conv_w, conv_b, w_a, w_b, w_o, ln_g, ln_b):
    bsz, t_len, _ = x.shape
    proj = jnp.einsum('bsd,de->bse', x, w_in) + b_in
    q, k, v, z_a, b_g, c_g, h, z_b, g_a, g_b = jnp.split(proj, _split_sizes(), axis=-1)
    q = q.reshape(bsz, t_len, N_HEADS, HEAD_DIM)
    k = k.reshape(bsz, t_len, N_HEADS, HEAD_DIM)
    v = v.reshape(bsz, t_len, N_HEADS, HEAD_DIM)

    if past_k is None:
        blocks = []
        for i in range(t_len // Q_BLOCK):
            t0, t1 = i * Q_BLOCK, (i + 1) * Q_BLOCK
            blocks.append(_stick_breaking(q[:, t0:t1], k[:, :t1], v[:, :t1],
                                          jnp.arange(t0, t1), jnp.arange(t1)))
        o = jnp.concatenate(blocks, axis=1)
    else:
        p_len = past_k.shape[1]
        k_all = jnp.concatenate([past_k, k], axis=1)
        v_all = jnp.concatenate([past_v, v], axis=1)
        o = _stick_breaking(q, k_all, v_all, p_len + jnp.arange(t_len), jnp.arange(p_len + t_len))
    y_a = o.reshape(bsz, t_len, W_ATT) * jax.nn.silu(z_a)
    p_a = jnp.einsum('bsc,cd->bsd', y_a, w_a)

    u = c_g * h
    if past_conv is None:
        past_conv = jnp.zeros((bsz, CONV_WIDTH - 1, W_CONV), u.dtype)
    full = jnp.concatenate([past_conv, u], axis=1)
    conv = conv_b + sum(full[:, i:i + t_len] * conv_w[i] for i in range(CONV_WIDTH))
    new_conv = full[:, -(CONV_WIDTH - 1):]
    y_b = jax.nn.silu(z_b) * b_g * conv
    p_b = jnp.einsum('bsc,cd->bsd', y_b, w_b)

    merged = jax.nn.sigmoid(g_a) * p_a + jax.nn.sigmoid(g_b) * p_b
    sub = jnp.einsum('bsd,de->bse', merged, w_o)
    y = _layer_norm(DN_ALPHA * x + sub, ln_g, ln_b)
    return y, k, v, new_conv


def setup_inputs(seed: int = 0) -> dict:
    key = jax.random.key(seed)
    ks = jax.random.split(key, 14)
    n_in = 4 * W_ATT + 4 * W_CONV + 2 * D_MODEL
    f32 = jnp.float32
    return {
        'x_prompt': jax.random.normal(ks[0], (BATCH, SEQ, D_MODEL), f32),
        'x_sample': jax.random.normal(ks[1], (DEC_BATCH, DEC_SEQ, D_MODEL), f32),
        'cache_k': jax.random.normal(ks[2], (DEPTH, DEC_BATCH, PAST_LEN, N_HEADS, HEAD_DIM), f32),
        'cache_v': jax.random.normal(ks[3], (DEPTH, DEC_BATCH, PAST_LEN, N_HEADS, HEAD_DIM), f32),
        'state_conv': jax.random.normal(ks[4], (DEPTH, DEC_BATCH, CONV_WIDTH - 1, W_CONV), f32),
        'w_in': jax.random.normal(ks[5], (DEPTH, D_MODEL, n_in), f32) * D_MODEL ** -0.5,
        'b_in': jax.random.normal(ks[6], (DEPTH, n_in), f32) * 0.02,
        'conv_w': jax.random.normal(ks[7], (DEPTH, CONV_WIDTH, W_CONV), f32) * CONV_WIDTH ** -0.5,
        'conv_b': jax.random.normal(ks[8], (DEPTH, W_CONV), f32) * 0.02,
        'w_a': jax.random.normal(ks[9], (DEPTH, W_ATT, D_MODEL), f32) * (W_ATT ** -0.5 * DN_BETA),
        'w_b': jax.random.normal(ks[10], (DEPTH, W_CONV, D_MODEL), f32) * (W_CONV ** -0.5 * DN_BETA),
        'w_o': jax.random.normal(ks[11], (DEPTH, D_MODEL, D_MODEL), f32) * (D_MODEL ** -0.5 * DN_BETA),
        'ln_g': 1.0 + 0.02 * jax.random.normal(ks[12], (DEPTH, D_MODEL), f32),
        'ln_b': 0.02 * jax.random.normal(ks[13], (DEPTH, D_MODEL), f32),
    }


def reference(x_prompt, x_sample, cache_k, cache_v, state_conv, w_in, b_in, conv_w, conv_b,
              w_a, w_b, w_o, ln_g, ln_b):
    xp, xs = x_prompt, x_sample
    kp, vp, cp, kn, vn, cn = [], [], [], [], [], []
    for l in range(DEPTH):
        wts = (w_in[l], b_in[l], conv_w[l], conv_b[l], w_a[l], w_b[l], w_o[l], ln_g[l], ln_b[l])
        xp, k1, v1, c1 = _layer(xp, None, None, None, *wts)
        xs, k2, v2, c2 = _layer(xs, cache_k[l], cache_v[l], state_conv[l], *wts)
        kp.append(k1); vp.append(v1); cp.append(c1)
        kn.append(k2); vn.append(v2); cn.append(c2)
    return (xp, xs, jnp.stack(kp), jnp.stack(vp), jnp.stack(cp), jnp.stack(kn), jnp.stack(vn), jnp.stack(cn))
```

```python
import functools

import jax
import jax.numpy as jnp
from jax import lax
from jax.experimental import pallas as pl
from jax.experimental.pallas import tpu as pltpu

F32 = jnp.float32
BF16 = jnp.bfloat16

LN_EPS = 1e-5
CONV_WIDTH = 3
VMEM_LIMIT_BYTES = 56 * 1024 * 1024
LANES = 128
ATT_BLOCK = 256


def _params(n_grid_dims):
    return pltpu.CompilerParams(
        dimension_semantics=("arbitrary",) * n_grid_dims,
        vmem_limit_bytes=VMEM_LIMIT_BYTES,
    )


def _sigmoid(x):
    return 1.0 / (1.0 + jnp.exp(-x))


def _silu(x):
    return x * _sigmoid(x)


def _proj_body(*refs, n_slabs, epilogue):
    x_ref = refs[0]
    w_refs = refs[1:1 + n_slabs]
    b_refs = refs[1 + n_slabs:1 + 2 * n_slabs]
    o_refs = refs[1 + 2 * n_slabs:]
    x = x_ref[...]
    accs = [jnp.dot(x, w[...], preferred_element_type=F32) + b[...]
            for w, b in zip(w_refs, b_refs)]
    for o_ref, val in zip(o_refs, epilogue(*accs)):
        o_ref[...] = val.astype(o_ref.dtype)


def _proj(x, w, b, col_offsets, width, epilogue, out_dtypes, *, tm, tn, name):
    m, k = x.shape
    n_slabs = len(col_offsets)
    tn = min(tn, width)
    grid = (m // tm, width // tn)
    in_specs = [pl.BlockSpec((tm, k), lambda i, j: (i, 0))]
    for off in col_offsets:
        in_specs.append(pl.BlockSpec((k, tn), lambda i, j, o=off // tn: (0, o + j)))
    for off in col_offsets:
        in_specs.append(pl.BlockSpec((1, tn), lambda i, j, o=off // tn: (0, o + j)))
    out_specs = [pl.BlockSpec((tm, tn), lambda i, j: (i, j)) for _ in out_dtypes]
    out_shape = [jax.ShapeDtypeStruct((m, width), dt) for dt in out_dtypes]
    return pl.pallas_call(
        functools.partial(_proj_body, n_slabs=n_slabs, epilogue=epilogue),
        grid=grid,
        in_specs=in_specs,
        out_specs=out_specs,
        out_shape=out_shape,
        compiler_params=_params(2),
        name=name,
    )(x, *([w] * n_slabs), *([b] * n_slabs))


def _suffix_ones(n):
    j = lax.broadcasted_iota(jnp.int32, (n, n), 0)
    s = lax.broadcasted_iota(jnp.int32, (n, n), 1)
    return (j > s).astype(BF16)


def _sb_block(q, k, v, u_mat, acc, carry, mask):
    z = lax.dot_general(q, k, (((1,), (1,)), ((), ())), preferred_element_type=F32)
    ls = jnp.minimum(z, 0.0) - jnp.log(1.0 + jnp.exp(-jnp.abs(z)))
    lk = ls - z
    if mask is not None:
        lk = jnp.where(mask, lk, 0.0)
    hi = lk.astype(BF16)
    lo = (lk - hi.astype(F32)).astype(BF16)
    after = (jnp.dot(hi, u_mat, preferred_element_type=F32)
             + jnp.dot(lo, u_mat, preferred_element_type=F32))
    w = jnp.exp(ls + after + carry)
    if mask is not None:
        w = jnp.where(mask, w, 0.0)
    acc = acc + jnp.dot(w.astype(BF16), v, preferred_element_type=F32)
    carry = carry + jnp.sum(lk, axis=1, keepdims=True)
    return acc, carry


def _attn_prompt_body(q_ref, k_ref, v_ref, za_ref, o_ref, kb_ref, vb_ref, *, heads, dh, blk):
    t_len = q_ref.shape[1]
    kb_ref[...] = k_ref[0].astype(BF16)
    vb_ref[...] = v_ref[0].astype(BF16)
    u_mat = _suffix_ones(blk)
    row = lax.broadcasted_iota(jnp.int32, (blk, blk), 0)
    col = lax.broadcasted_iota(jnp.int32, (blk, blk), 1)
    diag_mask = col < row

    def q_loop(qi, _):
        q0 = pl.multiple_of(qi * blk, blk)
        qs = [q_ref[0, pl.ds(q0, blk), h * dh:(h + 1) * dh] for h in range(heads)]
        state = []
        for h in range(heads):
            hs = slice(h * dh, (h + 1) * dh)
            state.append(_sb_block(
                qs[h], kb_ref[pl.ds(q0, blk), hs], vb_ref[pl.ds(q0, blk), hs], u_mat,
                jnp.zeros((blk, dh), F32), jnp.zeros((blk, 1), F32), diag_mask))

        def k_loop(t, st):
            k0 = pl.multiple_of((qi - 1 - t) * blk, blk)
            new = []
            for h in range(heads):
                hs = slice(h * dh, (h + 1) * dh)
                acc, carry = st[h]
                new.append(_sb_block(
                    qs[h], kb_ref[pl.ds(k0, blk), hs], vb_ref[pl.ds(k0, blk), hs], u_mat,
                    acc, carry, None))
            return tuple(new)

        state = lax.fori_loop(0, qi, k_loop, tuple(state))
        for h in range(heads):
            hs = slice(h * dh, (h + 1) * dh)
            gate = za_ref[0, pl.ds(q0, blk), hs].astype(F32)
            o_ref[0, pl.ds(q0, blk), hs] = (state[h][0] * gate).astype(o_ref.dtype)
        return 0

    lax.fori_loop(0, t_len // blk, q_loop, 0)


def _attn_prompt(q, k, v, za, *, dh, heads_per_step, blk):
    bsz, t_len, width = q.shape
    hw = heads_per_step * dh
    spec = pl.BlockSpec((1, t_len, hw), lambda b, g: (b, 0, g))
    return pl.pallas_call(
        functools.partial(_attn_prompt_body, heads=heads_per_step, dh=dh, blk=blk),
        grid=(bsz, width // hw),
        in_specs=[spec, spec, spec, spec],
        out_specs=spec,
        out_shape=jax.ShapeDtypeStruct((bsz, t_len, width), BF16),
        scratch_shapes=[pltpu.VMEM((t_len, hw), BF16), pltpu.VMEM((t_len, hw), BF16)],
        compiler_params=_params(2),
        name="attn_prompt",
    )(q, k, v, za)


def _attn_sample_body(q_ref, ck_ref, cv_ref, kn_ref, vn_ref, za_ref, o_ref, *, heads, dh, blk):
    t_len = q_ref.shape[1]
    p_len = ck_ref.shape[1]
    u_mat = _suffix_ones(blk)
    u_new = _suffix_ones(t_len)
    row = lax.broadcasted_iota(jnp.int32, (t_len, t_len), 0)
    col = lax.broadcasted_iota(jnp.int32, (t_len, t_len), 1)
    new_mask = col < row
    for h in range(heads):
        hs = slice(h * dh, (h + 1) * dh)
        q = q_ref[0, :, hs]
        acc, carry = _sb_block(
            q, kn_ref[0, :, hs].astype(BF16), vn_ref[0, :, hs].astype(BF16), u_new,
            jnp.zeros((t_len, dh), F32), jnp.zeros((t_len, 1), F32), new_mask)
        for c in range(p_len // blk - 1, -1, -1):
            rs = slice(c * blk, (c + 1) * blk)
            acc, carry = _sb_block(
                q, ck_ref[0, rs, hs].astype(BF16), cv_ref[0, rs, hs].astype(BF16), u_mat,
                acc, carry, None)
        o_ref[0, :, hs] = (acc * za_ref[0, :, hs].astype(F32)).astype(o_ref.dtype)


def _attn_sample(q, ck, cv, kn, vn, za, *, dh, heads_per_step, blk):
    bsz, t_len, width = q.shape
    p_len = ck.shape[1]
    hw = heads_per_step * dh
    new_spec = pl.BlockSpec((1, t_len, hw), lambda b, g: (b, 0, g))
    past_spec = pl.BlockSpec((1, p_len, hw), lambda b, g: (b, 0, g))
    return pl.pallas_call(
        functools.partial(_attn_sample_body, heads=heads_per_step, dh=dh, blk=blk),
        grid=(bsz, width // hw),
        in_specs=[new_spec, past_spec, past_spec, new_spec, new_spec, new_spec],
        out_specs=new_spec,
        out_shape=jax.ShapeDtypeStruct((bsz, t_len, width), BF16),
        compiler_params=_params(2),
        name="attn_sample",
    )(q, ck, cv, kn, vn, za)


def _conv_body(u_ref, g_ref, st_ref, cw_ref, cb_ref, y_ref, ns_ref):
    u = u_ref[0]
    st = st_ref[0]
    t_len = u.shape[0]
    row = lax.broadcasted_iota(jnp.int32, u.shape, 0)
    u1 = jnp.where(row == 0, st[1:2, :], pltpu.roll(u, 1, axis=0))
    u2 = jnp.where(row == 0, st[0:1, :],
                   jnp.where(row == 1, st[1:2, :], pltpu.roll(u, 2, axis=0)))
    conv = cb_ref[...] + u2 * cw_ref[0:1, :] + u1 * cw_ref[1:2, :] + u * cw_ref[2:3, :]
    y_ref[0] = (g_ref[0].astype(F32) * conv).astype(y_ref.dtype)
    ns_ref[0] = u_ref[0, t_len - (CONV_WIDTH - 1):, :]


def _conv(u, gate, state, conv_w, conv_b, *, tc):
    bsz, t_len, ch = u.shape
    tc = min(tc, ch)
    seq_spec = pl.BlockSpec((1, t_len, tc), lambda b, c: (b, 0, c))
    st_spec = pl.BlockSpec((1, CONV_WIDTH - 1, tc), lambda b, c: (b, 0, c))
    return pl.pallas_call(
        _conv_body,
        grid=(bsz, ch // tc),
        in_specs=[seq_spec, seq_spec, st_spec,
                  pl.BlockSpec((CONV_WIDTH, tc), lambda b, c: (0, c)),
                  pl.BlockSpec((1, tc), lambda b, c: (0, c))],
        out_specs=[seq_spec, st_spec],
        out_shape=[jax.ShapeDtypeStruct((bsz, t_len, ch), BF16),
                   jax.ShapeDtypeStruct((bsz, CONV_WIDTH - 1, ch), F32)],
        compiler_params=_params(2),
        name="short_conv",
    )(u, gate, state, conv_w, conv_b)


def _merge_body(ya_ref, yb_ref, wa_ref, wb_ref, ga_ref, gb_ref, o_ref):
    p_a = jnp.dot(ya_ref[...], wa_ref[...], preferred_element_type=F32)
    p_b = jnp.dot(yb_ref[...], wb_ref[...], preferred_element_type=F32)
    merged = ga_ref[...].astype(F32) * p_a + gb_ref[...].astype(F32) * p_b
    o_ref[...] = merged.astype(o_ref.dtype)


def _merge(y_a, y_b, w_a, w_b, gates, *, tm, tn):
    m, ka = y_a.shape
    kb = y_b.shape[1]
    d = w_a.shape[1]
    tn = min(tn, d)
    nj = d // tn
    return pl.pallas_call(
        _merge_body,
        grid=(m // tm, nj),
        in_specs=[pl.BlockSpec((tm, ka), lambda i, j: (i, 0)),
                  pl.BlockSpec((tm, kb), lambda i, j: (i, 0)),
                  pl.BlockSpec((ka, tn), lambda i, j: (0, j)),
                  pl.BlockSpec((kb, tn), lambda i, j: (0, j)),
                  pl.BlockSpec((tm, tn), lambda i, j: (i, j)),
                  pl.BlockSpec((tm, tn), lambda i, j: (i, nj + j))],
        out_specs=pl.BlockSpec((tm, tn), lambda i, j: (i, j)),
        out_shape=jax.ShapeDtypeStruct((m, d), BF16),
        compiler_params=_params(2),
        name="gated_merge",
    )(y_a, y_b, w_a, w_b, gates, gates)


def _out_body(m_ref, w_ref, x_ref, g_ref, b_ref, o_ref, *, alpha, n_blk):
    j = pl.program_id(1)
    tm, d = o_ref.shape
    tn = d // n_blk
    sub = jnp.dot(m_ref[...], w_ref[...], preferred_element_type=F32)
    for c in range(n_blk):
        @pl.when(j == c)
        def _(c=c):
            o_ref[:, c * tn:(c + 1) * tn] = sub

    @pl.when(j == n_blk - 1)
    def _():
        s1 = jnp.zeros((tm, 1), F32)
        for c in range(n_blk):
            cs = slice(c * tn, (c + 1) * tn)
            pre = alpha * x_ref[:, cs] + o_ref[:, cs]
            o_ref[:, cs] = pre
            s1 = s1 + jnp.sum(pre, axis=1, keepdims=True)
        mu = s1 / d
        s2 = jnp.zeros((tm, 1), F32)
        for c in range(n_blk):
            cen = o_ref[:, c * tn:(c + 1) * tn] - mu
            s2 = s2 + jnp.sum(cen * cen, axis=1, keepdims=True)
        inv = lax.rsqrt(s2 / d + LN_EPS)
        for c in range(n_blk):
            cs = slice(c * tn, (c + 1) * tn)
            o_ref[:, cs] = (o_ref[:, cs] - mu) * inv * g_ref[:, cs] + b_ref[:, cs]


def _out_proj_ln(merged, w_o, x, ln_g, ln_b, *, alpha, tm, tn):
    m, d = x.shape
    tm, tn = min(tm, m), min(tn, d)
    nj = d // tn
    row_spec = lambda: pl.BlockSpec((tm, d), lambda i, j: (i, 0))
    vec_spec = lambda: pl.BlockSpec((1, d), lambda i, j: (0, 0))
    return pl.pallas_call(
        functools.partial(_out_body, alpha=alpha, n_blk=nj),
        grid=(m // tm, nj),
        in_specs=[row_spec(), pl.BlockSpec((d, tn), lambda i, j: (0, j)), row_spec(),
                  vec_spec(), vec_spec()],
        out_specs=row_spec(),
        out_shape=jax.ShapeDtypeStruct((m, d), F32),
        compiler_params=_params(2),
        name="out_proj_ln",
    )(merged, w_o, x, ln_g, ln_b)


def _layer(x, past_k, past_v, past_conv, w_in, b_in, conv_w, conv_b, w_a, w_b, w_o, ln_g, ln_b,
           *, n_heads, dh, alpha):
    bsz, t_len, d = x.shape
    m = bsz * t_len
    w_att = n_heads * dh
    w_conv = conv_w.shape[1]
    tm = min(1024, m)
    xb = x.reshape(m, d).astype(BF16)
    scale = dh ** -0.5

    o_q, o_k, o_v, o_za = 0, w_att, 2 * w_att, 3 * w_att
    o_bg = 4 * w_att
    o_cg, o_h, o_zb = o_bg + w_conv, o_bg + 2 * w_conv, o_bg + 3 * w_conv
    o_g = o_bg + 4 * w_conv
    proj = functools.partial(_proj, xb, w_in, b_in, tm=tm)
    (q,) = proj([o_q], w_att, lambda a: (a * scale,), [BF16], tn=1024, name="proj_q")
    k, v = proj([o_k, o_v], w_att, lambda a, b: (a, b), [F32, F32], tn=512, name="proj_kv")
    (za,) = proj([o_za], w_att, lambda a: (_silu(a),), [BF16], tn=1024, name="proj_za")
    (gate_b,) = proj([o_bg, o_zb], w_conv, lambda bg, zb: (_silu(zb) * bg,), [BF16], tn=512,
                     name="proj_gate_b")
    (u,) = proj([o_cg, o_h], w_conv, lambda cg, h: (cg * h,), [F32], tn=512, name="proj_u")
    (gates,) = proj([o_g], 2 * d, lambda a: (_sigmoid(a),), [BF16], tn=1024, name="proj_gates")

    seq = lambda a: a.reshape(bsz, t_len, a.shape[-1])
    if past_k is None:
        y_a = _attn_prompt(seq(q), seq(k), seq(v), seq(za), dh=dh, heads_per_step=2,
                           blk=ATT_BLOCK)
        past_conv = jnp.zeros((bsz, CONV_WIDTH - 1, w_conv), F32)
        tc = 512
    else:
        p_len = past_k.shape[1]
        y_a = _attn_sample(seq(q), past_k.reshape(bsz, p_len, w_att),
                           past_v.reshape(bsz, p_len, w_att), seq(k), seq(v), seq(za),
                           dh=dh, heads_per_step=4, blk=LANES)
        tc = w_conv
    y_b, new_conv = _conv(seq(u), seq(gate_b), past_conv, conv_w, conv_b, tc=tc)

    merged = _merge(y_a.reshape(m, w_att), y_b.reshape(m, w_conv), w_a, w_b, gates,
                    tm=tm, tn=1024)
    y = _out_proj_ln(merged, w_o, x.reshape(m, d), ln_g, ln_b, alpha=alpha, tm=256, tn=1024)
    return (y.reshape(bsz, t_len, d), k.reshape(bsz, t_len, n_heads, dh),
            v.reshape(bsz, t_len, n_heads, dh), new_conv)


def kernel(x_prompt, x_sample, cache_k, cache_v, state_conv, w_in, b_in, conv_w, conv_b,
           w_a, w_b, w_o, ln_g, ln_b):
    depth = w_in.shape[0]
    n_heads, dh = cache_k.shape[-2:]
    alpha = (2.0 * depth) ** 0.25
    xp, xs = x_prompt, x_sample
    kp, vp, cp, kn, vn, cn = [], [], [], [], [], []
    for l in range(depth):
        wts = (w_in[l].astype(BF16), b_in[l][None, :], conv_w[l], conv_b[l][None, :],
               w_a[l].astype(BF16), w_b[l].astype(BF16), w_o[l].astype(BF16),
               ln_g[l][None, :], ln_b[l][None, :])
        layer = functools.partial(_layer, n_heads=n_heads, dh=dh, alpha=alpha)
        xp, k1, v1, c1 = layer(xp, None, None, None, *wts)
        xs, k2, v2, c2 = layer(xs, cache_k[l], cache_v[l], state_conv[l], *wts)
        kp.append(k1); vp.append(v1); cp.append(c1)
        kn.append(k2); vn.append(v2); cn.append(c2)
    return (xp, xs, jnp.stack(kp), jnp.stack(vp), jnp.stack(cp),
            jnp.stack(kn), jnp.stack(vn), jnp.stack(cn))
```

```python
import functools
import math

import jax
import jax.numpy as jnp
from jax import lax
from jax.experimental import pallas as pl
from jax.experimental.pallas import tpu as pltpu

F32 = jnp.float32
BF16 = jnp.bfloat16

LN_EPS = 1e-5
CONV_WIDTH = 3
VMEM_LIMIT_BYTES = 56 * 1024 * 1024
LANES = 128
ATT_BLOCK = 256
MASKED_SCORE = -1e30
NT_DIMS = (((1,), (1,)), ((), ()))


def _params(n_grid_dims):
    return pltpu.CompilerParams(
        dimension_semantics=("arbitrary",) * n_grid_dims,
        vmem_limit_bytes=VMEM_LIMIT_BYTES,
    )


def _sigmoid(x):
    return 1.0 / (1.0 + jnp.exp(-x))


def _silu(x):
    return x * _sigmoid(x)


def _proj_body(*refs, n_slabs, epilogue):
    x_ref = refs[0]
    w_refs = refs[1:1 + n_slabs]
    b_refs = refs[1 + n_slabs:1 + 2 * n_slabs]
    o_refs = refs[1 + 2 * n_slabs:]
    x = x_ref[...]
    accs = [jnp.dot(x, w[...], preferred_element_type=F32) + b[...]
            for w, b in zip(w_refs, b_refs)]
    for o_ref, val in zip(o_refs, epilogue(*accs)):
        o_ref[...] = val.astype(o_ref.dtype)


def _proj(x, w, b, col_offsets, width, epilogue, out_dtypes, *, tm, tn, name):
    m, k = x.shape
    n_slabs = len(col_offsets)
    tn = min(tn, width)
    grid = (m // tm, width // tn)
    in_specs = [pl.BlockSpec((tm, k), lambda i, j: (i, 0))]
    for off in col_offsets:
        in_specs.append(pl.BlockSpec((k, tn), lambda i, j, o=off // tn: (0, o + j)))
    for off in col_offsets:
        in_specs.append(pl.BlockSpec((1, tn), lambda i, j, o=off // tn: (0, o + j)))
    out_specs = [pl.BlockSpec((tm, tn), lambda i, j: (i, j)) for _ in out_dtypes]
    out_shape = [jax.ShapeDtypeStruct((m, width), dt) for dt in out_dtypes]
    return pl.pallas_call(
        functools.partial(_proj_body, n_slabs=n_slabs, epilogue=epilogue),
        grid=grid,
        in_specs=in_specs,
        out_specs=out_specs,
        out_shape=out_shape,
        compiler_params=_params(2),
        name=name,
    )(x, *([w] * n_slabs), *([b] * n_slabs))


def _proj_heads_body(x_ref, w_ref, b_ref, of_ref, ob_ref):
    acc = jnp.dot(x_ref[...], w_ref[...], preferred_element_type=F32) + b_ref[...]
    ob_ref[...] = acc.astype(ob_ref.dtype)
    _, heads, dh = of_ref.shape
    for h in range(heads):
        of_ref[:, h, :] = acc[:, h * dh:(h + 1) * dh]


def _proj_heads(x, w, b, col_offset, width, dh, *, tm, tn, name):
    m, k = x.shape
    tn = min(tn, width)
    o = col_offset // tn
    return pl.pallas_call(
        _proj_heads_body,
        grid=(m // tm, width // tn),
        in_specs=[pl.BlockSpec((tm, k), lambda i, j: (i, 0)),
                  pl.BlockSpec((k, tn), lambda i, j: (0, o + j)),
                  pl.BlockSpec((1, tn), lambda i, j: (0, o + j))],
        out_specs=[pl.BlockSpec((tm, tn // dh, dh), lambda i, j: (i, j, 0)),
                   pl.BlockSpec((tm, tn), lambda i, j: (i, j))],
        out_shape=[jax.ShapeDtypeStruct((m, width // dh, dh), F32),
                   jax.ShapeDtypeStruct((m, width), BF16)],
        compiler_params=_params(2),
        name=name,
    )(x, w, b)


def _suffix_ones(n):
    j = lax.broadcasted_iota(jnp.int32, (n, n), 0)
    s = lax.broadcasted_iota(jnp.int32, (n, n), 1)
    return (j > s).astype(BF16)


def _causal_bias(n):
    row = lax.broadcasted_iota(jnp.int32, (n, n), 0)
    col = lax.broadcasted_iota(jnp.int32, (n, n), 1)
    return jnp.where(col < row, 0.0, MASKED_SCORE).astype(F32)


def _sb_logs(z):
    ls = jnp.minimum(z, 0.0) - jnp.log2(1.0 + jnp.exp2(-jnp.abs(z)))
    return ls, ls - z


def _add_lane_tiled(a, r):
    return jnp.concatenate(
        [a[:, c:c + LANES] + r for c in range(0, a.shape[1], LANES)], axis=1)


def _sb_suffix(lk, u_mat):
    return jnp.dot(lk.astype(BF16), u_mat, preferred_element_type=F32)


def _attn_prompt_body(q_ref, k_ref, v_ref, za_ref, o_ref,
                      bias_ref, z_buf, tw_buf, af_buf, run_ref, acc_ref, *, heads, dh, blk):
    t_len = q_ref.shape[1]
    nq = t_len // blk
    n_pairs = nq * (nq + 1) // 2
    u_mat = _suffix_ones(blk)
    bias_ref[0:blk, :] = _causal_bias(blk)
    bias_ref[blk:2 * blk, :] = jnp.zeros((blk, blk), F32)
    hsl = [slice(h * dh, (h + 1) * dh) for h in range(heads)]
    for h in range(heads):
        z_buf[1, h] = jnp.full((blk, blk), MASKED_SCORE, F32)
        tw_buf[1, h] = jnp.full((blk, blk), MASKED_SCORE, F32)
        af_buf[1, h] = jnp.zeros((blk, blk), F32)
        run_ref[h] = jnp.zeros((blk, LANES), F32)
        acc_ref[h] = jnp.zeros((blk, dh), F32)

    def next_pair(qi, kb):
        wrap = kb == 0
        return jnp.where(wrap, qi + 1, qi), jnp.where(wrap, qi + 1, kb - 1)

    def step(st, new):
        (qa, ka), (qb, kb), (qc, kc) = st
        old = 1 - new
        q0 = pl.multiple_of(jnp.minimum(qa, nq - 1) * blk, blk)
        k0 = pl.multiple_of(jnp.minimum(ka, nq - 1) * blk, blk)
        b0 = pl.multiple_of(jnp.where(ka == qa, 0, blk), blk)
        bias = bias_ref[pl.ds(b0, blk), :]
        z_new = [lax.dot_general(q_ref[0, pl.ds(q0, blk), hsl[h]],
                                 k_ref[0, pl.ds(k0, blk), hsl[h]], NT_DIMS,
                                 preferred_element_type=F32) for h in range(heads)]
        first = kb == qb
        af_new = []
        for h in range(heads):
            ls, lk = _sb_logs(z_buf[old, h])
            off = jnp.where(first, 0.0, run_ref[h])
            tw_buf[new, h] = _add_lane_tiled(ls, off)
            af_new.append(_sb_suffix(lk, u_mat))
            run_ref[h] = off + jnp.broadcast_to(jnp.sum(lk, axis=1, keepdims=True), off.shape)
        vc0 = pl.multiple_of(kc * blk, blk)
        qc0 = pl.multiple_of(qc * blk, blk)
        last = kc == 0
        for h in range(heads):
            w = jnp.exp2(tw_buf[old, h] + af_buf[old, h]).astype(BF16)
            acc = acc_ref[h] + jnp.dot(w, v_ref[0, pl.ds(vc0, blk), hsl[h]],
                                       preferred_element_type=F32)
            gate = za_ref[0, pl.ds(qc0, blk), hsl[h]].astype(F32)
            o_ref[0, pl.ds(qc0, blk), hsl[h]] = (acc * gate).astype(o_ref.dtype)
            acc_ref[h] = jnp.where(last, 0.0, acc)
        for h in range(heads):
            z_buf[new, h] = z_new[h] + bias
            af_buf[new, h] = af_new[h]
        return next_pair(qa, ka), (qa, ka), (qb, kb)

    zero = jnp.int32(0)
    n_steps = n_pairs + 2
    st = lax.fori_loop(0, n_steps // 2, lambda _, s: step(step(s, 0), 1),
                       ((zero, zero), (zero, zero), (zero, zero)))
    if n_steps % 2:
        step(st, 0)


def _attn_prompt(q, k, v, za, *, dh, heads_per_step, blk):
    bsz, t_len, width = q.shape
    hw = heads_per_step * dh
    spec = pl.BlockSpec((1, t_len, hw), lambda b, g: (b, 0, g))
    return pl.pallas_call(
        functools.partial(_attn_prompt_body, heads=heads_per_step, dh=dh, blk=blk),
        grid=(bsz, width // hw),
        in_specs=[spec, spec, spec, spec],
        out_specs=spec,
        out_shape=jax.ShapeDtypeStruct((bsz, t_len, width), BF16),
        scratch_shapes=[pltpu.VMEM((2 * blk, blk), F32),
                        pltpu.VMEM((2, heads_per_step, blk, blk), F32),
                        pltpu.VMEM((2, heads_per_step, blk, blk), F32),
                        pltpu.VMEM((2, heads_per_step, blk, blk), F32),
                        pltpu.VMEM((heads_per_step, blk, LANES), F32),
                        pltpu.VMEM((heads_per_step, blk, dh), F32)],
        compiler_params=_params(2),
        name="attn_prompt",
    )(q, k, v, za)


def _attn_sample_body(q_ref, ck_ref, cv_ref, kn_ref, vn_ref, za_ref, o_ref,
                      *, heads, n_heads, dh, chunk):
    g = pl.program_id(1)
    t_len = q_ref.shape[1]
    p_len = ck_ref.shape[1] // n_heads
    n_chunks = p_len // chunk
    u_mat = _suffix_ones(chunk)
    u_new = _suffix_ones(t_len)
    bias_new = _causal_bias(t_len)
    hsl = [slice(h * dh, (h + 1) * dh) for h in range(heads)]

    def past_rows(ref, h):
        return ref[0, pl.ds(g * heads + h, p_len, stride=n_heads), :].astype(BF16)

    z_past, z_new = [], []
    for h in range(heads):
        q = q_ref[0, :, hsl[h]]
        z_past.append(lax.dot_general(q, past_rows(ck_ref, h), NT_DIMS,
                                      preferred_element_type=F32))
        z_new.append(lax.dot_general(q, kn_ref[0, :, hsl[h]], NT_DIMS,
                                     preferred_element_type=F32) + bias_new)
    z_past = jnp.concatenate(z_past, axis=0)
    z_new = jnp.concatenate(z_new, axis=0)

    ls_new, lk_new = _sb_logs(z_new)
    w_new = jnp.exp2(ls_new + _sb_suffix(lk_new, u_new)).astype(BF16)
    off = jnp.sum(lk_new, axis=1, keepdims=True)
    ls_past, lk_past = _sb_logs(z_past)
    w_chunks = [None] * n_chunks
    for c in range(n_chunks - 1, -1, -1):
        cs = slice(c * chunk, (c + 1) * chunk)
        lk = lk_past[:, cs]
        w_chunks[c] = jnp.exp2(ls_past[:, cs] + _sb_suffix(lk, u_mat) + off).astype(BF16)
        off = off + jnp.sum(lk, axis=1, keepdims=True)
    w_past = jnp.concatenate(w_chunks, axis=1)

    for h in range(heads):
        rows = slice(h * t_len, (h + 1) * t_len)
        acc = (jnp.dot(w_past[rows], past_rows(cv_ref, h), preferred_element_type=F32)
               + jnp.dot(w_new[rows], vn_ref[0, :, hsl[h]], preferred_element_type=F32))
        o_ref[0, :, hsl[h]] = (acc * za_ref[0, :, hsl[h]].astype(F32)).astype(o_ref.dtype)


def _attn_sample(q, ck, cv, kn, vn, za, *, n_heads, dh, heads_per_step, chunk):
    bsz, t_len, width = q.shape
    hw = heads_per_step * dh
    new_spec = pl.BlockSpec((1, t_len, hw), lambda b, g: (b, 0, g))
    past_spec = pl.BlockSpec((1, ck.shape[1], dh), lambda b, g: (b, 0, 0))
    return pl.pallas_call(
        functools.partial(_attn_sample_body, heads=heads_per_step, n_heads=n_heads, dh=dh,
                          chunk=chunk),
        grid=(bsz, width // hw),
        in_specs=[new_spec, past_spec, past_spec, new_spec, new_spec, new_spec],
        out_specs=new_spec,
        out_shape=jax.ShapeDtypeStruct((bsz, t_len, width), BF16),
        compiler_params=_params(2),
        name="attn_sample",
    )(q, ck, cv, kn, vn, za)


def _conv_body(u_ref, g_ref, st_ref, cw_ref, cb_ref, y_ref, ns_ref):
    u = u_ref[0]
    st = st_ref[0]
    t_len = u.shape[0]
    row = lax.broadcasted_iota(jnp.int32, u.shape, 0)
    u1 = jnp.where(row == 0, st[1:2, :], pltpu.roll(u, 1, axis=0))
    u2 = jnp.where(row == 0, st[0:1, :],
                   jnp.where(row == 1, st[1:2, :], pltpu.roll(u, 2, axis=0)))
    conv = cb_ref[...] + u2 * cw_ref[0:1, :] + u1 * cw_ref[1:2, :] + u * cw_ref[2:3, :]
    y_ref[0] = (g_ref[0].astype(F32) * conv).astype(y_ref.dtype)
    ns_ref[0] = u_ref[0, t_len - (CONV_WIDTH - 1):, :]


def _conv(u, gate, state, conv_w, conv_b, *, tc):
    bsz, t_len, ch = u.shape
    tc = min(tc, ch)
    seq_spec = pl.BlockSpec((1, t_len, tc), lambda b, c: (b, 0, c))
    st_spec = pl.BlockSpec((1, CONV_WIDTH - 1, tc), lambda b, c: (b, 0, c))
    return pl.pallas_call(
        _conv_body,
        grid=(bsz, ch // tc),
        in_specs=[seq_spec, seq_spec, st_spec,
                  pl.BlockSpec((CONV_WIDTH, tc), lambda b, c: (0, c)),
                  pl.BlockSpec((1, tc), lambda b, c: (0, c))],
        out_specs=[seq_spec, st_spec],
        out_shape=[jax.ShapeDtypeStruct((bsz, t_len, ch), BF16),
                   jax.ShapeDtypeStruct((bsz, CONV_WIDTH - 1, ch), F32)],
        compiler_params=_params(2),
        name="short_conv",
    )(u, gate, state, conv_w, conv_b)


def _merge_body(ya_ref, yb_ref, wa_ref, wb_ref, ga_ref, gb_ref, o_ref):
    p_a = jnp.dot(ya_ref[...], wa_ref[...], preferred_element_type=F32)
    p_b = jnp.dot(yb_ref[...], wb_ref[...], preferred_element_type=F32)
    merged = ga_ref[...].astype(F32) * p_a + gb_ref[...].astype(F32) * p_b
    o_ref[...] = merged.astype(o_ref.dtype)


def _merge(y_a, y_b, w_a, w_b, gates, *, tm, tn):
    m, ka = y_a.shape
    kb = y_b.shape[1]
    d = w_a.shape[1]
    tn = min(tn, d)
    nj = d // tn
    return pl.pallas_call(
        _merge_body,
        grid=(m // tm, nj),
        in_specs=[pl.BlockSpec((tm, ka), lambda i, j: (i, 0)),
                  pl.BlockSpec((tm, kb), lambda i, j: (i, 0)),
                  pl.BlockSpec((ka, tn), lambda i, j: (0, j)),
                  pl.BlockSpec((kb, tn), lambda i, j: (0, j)),
                  pl.BlockSpec((tm, tn), lambda i, j: (i, j)),
                  pl.BlockSpec((tm, tn), lambda i, j: (i, nj + j))],
        out_specs=pl.BlockSpec((tm, tn), lambda i, j: (i, j)),
        out_shape=jax.ShapeDtypeStruct((m, d), BF16),
        compiler_params=_params(2),
        name="gated_merge",
    )(y_a, y_b, w_a, w_b, gates, gates)


def _out_body(m_ref, w_ref, x_ref, g_ref, b_ref, o_ref, *, alpha):
    pre = alpha * x_ref[...] + jnp.dot(m_ref[...], w_ref[...], preferred_element_type=F32)
    mu = jnp.mean(pre, axis=1, keepdims=True)
    cen = pre - mu
    var = jnp.mean(cen * cen, axis=1, keepdims=True)
    o_ref[...] = cen * lax.rsqrt(var + LN_EPS) * g_ref[...] + b_ref[...]


def _out_proj_ln(merged, w_o, x, ln_g, ln_b, *, alpha, tm):
    m, d = x.shape
    tm = min(tm, m)
    row_spec = lambda: pl.BlockSpec((tm, d), lambda i: (i, 0))
    const_spec = lambda r: pl.BlockSpec((r, d), lambda i: (0, 0), pipeline_mode=pl.Buffered(1))
    return pl.pallas_call(
        functools.partial(_out_body, alpha=alpha),
        grid=(m // tm,),
        in_specs=[row_spec(), const_spec(d), row_spec(), const_spec(1), const_spec(1)],
        out_specs=row_spec(),
        out_shape=jax.ShapeDtypeStruct((m, d), F32),
        compiler_params=_params(1),
        name="out_proj_ln",
    )(merged, w_o, x, ln_g, ln_b)


def _layer(x, past_k, past_v, past_conv, w_in, b_in, conv_w, conv_b, w_a, w_b, w_o, ln_g, ln_b,
           *, n_heads, dh, alpha):
    bsz, t_len, d = x.shape
    m = bsz * t_len
    w_att = n_heads * dh
    w_conv = conv_w.shape[1]
    tm = min(1024, m)
    xb = x.reshape(m, d).astype(BF16)
    q_scale = dh ** -0.5 * math.log2(math.e)

    o_q, o_k, o_v, o_za = 0, w_att, 2 * w_att, 3 * w_att
    o_bg = 4 * w_att
    o_cg, o_h, o_zb = o_bg + w_conv, o_bg + 2 * w_conv, o_bg + 3 * w_conv
    o_g = o_bg + 4 * w_conv
    proj = functools.partial(_proj, xb, w_in, b_in, tm=tm)
    (q,) = proj([o_q], w_att, lambda a: (a * q_scale,), [BF16], tn=1024, name="proj_q")
    k, kb16 = _proj_heads(xb, w_in, b_in, o_k, w_att, dh, tm=tm, tn=1024, name="proj_k")
    v, vb16 = _proj_heads(xb, w_in, b_in, o_v, w_att, dh, tm=tm, tn=1024, name="proj_v")
    (za,) = proj([o_za], w_att, lambda a: (_silu(a),), [BF16], tn=1024, name="proj_za")
    (gate_b,) = proj([o_bg, o_zb], w_conv, lambda bg, zb: (_silu(zb) * bg,), [BF16], tn=512,
                     name="proj_gate_b")
    (u,) = proj([o_cg, o_h], w_conv, lambda cg, h: (cg * h,), [F32], tn=512, name="proj_u")
    (gates,) = proj([o_g], 2 * d, lambda a: (_sigmoid(a),), [BF16], tn=1024, name="proj_gates")

    seq = lambda a: a.reshape(bsz, t_len, a.shape[-1])
    if past_k is None:
        y_a = _attn_prompt(seq(q), seq(kb16), seq(vb16), seq(za), dh=dh, heads_per_step=2,
                           blk=ATT_BLOCK)
        past_conv = jnp.zeros((bsz, CONV_WIDTH - 1, w_conv), F32)
        tc = 512
    else:
        p_len = past_k.shape[1]
        y_a = _attn_sample(seq(q), past_k.reshape(bsz, p_len * n_heads, dh),
                           past_v.reshape(bsz, p_len * n_heads, dh), seq(kb16), seq(vb16),
                           seq(za), n_heads=n_heads, dh=dh, heads_per_step=4, chunk=ATT_BLOCK)
        tc = w_conv
    y_b, new_conv = _conv(seq(u), seq(gate_b), past_conv, conv_w, conv_b, tc=tc)

    merged = _merge(y_a.reshape(m, w_att), y_b.reshape(m, w_conv), w_a, w_b, gates,
                    tm=tm, tn=1024)
    y = _out_proj_ln(merged, w_o, x.reshape(m, d), ln_g, ln_b, alpha=alpha, tm=128)
    return (y.reshape(bsz, t_len, d), k.reshape(bsz, t_len, n_heads, dh),
            v.reshape(bsz, t_len, n_heads, dh), new_conv)


def kernel(x_prompt, x_sample, cache_k, cache_v, state_conv, w_in, b_in, conv_w, conv_b,
           w_a, w_b, w_o, ln_g, ln_b):
    depth = w_in.shape[0]
    n_heads, dh = cache_k.shape[-2:]
    alpha = (2.0 * depth) ** 0.25
    xp, xs = x_prompt, x_sample
    kp, vp, cp, kn, vn, cn = [], [], [], [], [], []
    for l in range(depth):
        wts = (w_in[l].astype(BF16), b_in[l][None, :], conv_w[l], conv_b[l][None, :],
               w_a[l].astype(BF16), w_b[l].astype(BF16), w_o[l].astype(BF16),
               ln_g[l][None, :], ln_b[l][None, :])
        layer = functools.partial(_layer, n_heads=n_heads, dh=dh, alpha=alpha)
        xp, k1, v1, c1 = layer(xp, None, None, None, *wts)
        xs, k2, v2, c2 = layer(xs, cache_k[l], cache_v[l], state_conv[l], *wts)
        kp.append(k1); vp.append(v1); cp.append(c1)
        kn.append(k2); vn.append(v2); cn.append(c2)
    return (xp, xs, jnp.stack(kp), jnp.stack(vp), jnp.stack(cp),
            jnp.stack(kn), jnp.stack(vn), jnp.stack(cn))
```

```python
import functools
import math

import jax
import jax.numpy as jnp
from jax import lax
from jax.experimental import pallas as pl
from jax.experimental.pallas import tpu as pltpu

F32 = jnp.float32
BF16 = jnp.bfloat16

LN_EPS = 1e-5
CONV_WIDTH = 3
VMEM_LIMIT_BYTES = 56 * 1024 * 1024
LANES = 128
ATT_BLOCK = 256
ROW_TILE = 1024
MASKED_SCORE = -1e30
NT_DIMS = (((1,), (1,)), ((), ()))


def _params(n_grid_dims):
    return pltpu.CompilerParams(
        dimension_semantics=("arbitrary",) * n_grid_dims,
        vmem_limit_bytes=VMEM_LIMIT_BYTES,
    )


def _sigmoid(x):
    return 1.0 / (1.0 + jnp.exp(-x))


def _silu(x):
    return x * _sigmoid(x)


def _proj_body(*refs, n_slabs, n_out, epilogue):
    x_ref = refs[0]
    w_refs = refs[1:1 + n_slabs]
    b_refs = refs[1 + n_slabs:1 + 2 * n_slabs]
    o_refs = refs[1 + 2 * n_slabs:1 + 2 * n_slabs + n_out]
    wb_refs = refs[1 + 2 * n_slabs + n_out:]

    @pl.when(pl.program_id(1) == 0)
    def _():
        for w, wb in zip(w_refs, wb_refs):
            wb[...] = w[...].astype(BF16)

    x = x_ref[...]
    accs = [jnp.dot(x, wb[...], preferred_element_type=F32) + b[...]
            for wb, b in zip(wb_refs, b_refs)]
    for o_ref, val in zip(o_refs, epilogue(*accs)):
        o_ref[...] = val.astype(o_ref.dtype)


def _proj(x, w, b, col_offsets, width, epilogue, out_dtypes, *, tm, tn, name):
    m, k = x.shape
    n_slabs = len(col_offsets)
    tn = min(tn, width)
    in_specs = [pl.BlockSpec((tm, k), lambda j, i: (i, 0))]
    for off in col_offsets:
        in_specs.append(pl.BlockSpec((k, tn), lambda j, i, o=off // tn: (0, o + j)))
    for off in col_offsets:
        in_specs.append(pl.BlockSpec((1, tn), lambda j, i, o=off // tn: (0, o + j)))
    return pl.pallas_call(
        functools.partial(_proj_body, n_slabs=n_slabs, n_out=len(out_dtypes), epilogue=epilogue),
        grid=(width // tn, m // tm),
        in_specs=in_specs,
        out_specs=[pl.BlockSpec((tm, tn), lambda j, i: (i, j)) for _ in out_dtypes],
        out_shape=[jax.ShapeDtypeStruct((m, width), dt) for dt in out_dtypes],
        scratch_shapes=[pltpu.VMEM((k, tn), BF16) for _ in col_offsets],
        compiler_params=_params(2),
        name=name,
    )(x, *([w] * n_slabs), *([b] * n_slabs))


def _proj_heads_body(x_ref, w_ref, b_ref, of_ref, ob_ref, wb_ref):
    @pl.when(pl.program_id(1) == 0)
    def _():
        wb_ref[...] = w_ref[...].astype(BF16)

    acc = jnp.dot(x_ref[...], wb_ref[...], preferred_element_type=F32) + b_ref[...]
    ob_ref[...] = acc.astype(ob_ref.dtype)
    _, heads, dh = of_ref.shape
    for h in range(heads):
        of_ref[:, h, :] = acc[:, h * dh:(h + 1) * dh]


def _proj_heads(x, w, b, col_offset, width, dh, *, row_offset, rows, tm, tn, name):
    k = x.shape[1]
    tn = min(tn, width)
    o = col_offset // tn
    r = row_offset // tm
    return pl.pallas_call(
        _proj_heads_body,
        grid=(width // tn, rows // tm),
        in_specs=[pl.BlockSpec((tm, k), lambda j, i: (r + i, 0)),
                  pl.BlockSpec((k, tn), lambda j, i: (0, o + j), pipeline_mode=pl.Buffered(1)),
                  pl.BlockSpec((1, tn), lambda j, i: (0, o + j))],
        out_specs=[pl.BlockSpec((tm, tn // dh, dh), lambda j, i: (i, j, 0)),
                   pl.BlockSpec((tm, tn), lambda j, i: (i, j))],
        out_shape=[jax.ShapeDtypeStruct((rows, width // dh, dh), F32),
                   jax.ShapeDtypeStruct((rows, width), BF16)],
        scratch_shapes=[pltpu.VMEM((k, tn), BF16)],
        compiler_params=_params(2),
        name=name,
    )(x, w, b)


def _suffix_ones(n):
    j = lax.broadcasted_iota(jnp.int32, (n, n), 0)
    s = lax.broadcasted_iota(jnp.int32, (n, n), 1)
    return (j > s).astype(BF16)


def _causal_bias(n):
    row = lax.broadcasted_iota(jnp.int32, (n, n), 0)
    col = lax.broadcasted_iota(jnp.int32, (n, n), 1)
    return jnp.where(col < row, 0.0, MASKED_SCORE).astype(F32)


def _sb_logs(z):
    ls = jnp.minimum(z, 0.0) - jnp.log2(1.0 + jnp.exp2(-jnp.abs(z)))
    return ls, ls - z


def _add_lane_tiled(a, r):
    return jnp.concatenate(
        [a[:, c:c + LANES] + r for c in range(0, a.shape[1], LANES)], axis=1)


def _sb_suffix(lk, u_mat):
    return jnp.dot(lk.astype(BF16), u_mat, preferred_element_type=F32)


def _attn_prompt_body(q_ref, k_ref, v_ref, za_ref, o_ref,
                      bias_ref, z_buf, tw_buf, af_buf, run_ref, acc_ref, *, heads, dh, blk):
    t_len = q_ref.shape[0]
    nq = t_len // blk
    n_pairs = nq * (nq + 1) // 2
    u_mat = _suffix_ones(blk)
    bias_ref[0:blk, :] = _causal_bias(blk)
    bias_ref[blk:2 * blk, :] = jnp.zeros((blk, blk), F32)
    hsl = [slice(h * dh, (h + 1) * dh) for h in range(heads)]
    for h in range(heads):
        z_buf[1, h] = jnp.full((blk, blk), MASKED_SCORE, F32)
        tw_buf[1, h] = jnp.full((blk, blk), MASKED_SCORE, F32)
        af_buf[1, h] = jnp.zeros((blk, blk), F32)
        run_ref[h] = jnp.zeros((blk, LANES), F32)
        acc_ref[h] = jnp.zeros((blk, dh), F32)

    def next_pair(qi, kb):
        wrap = kb == 0
        return jnp.where(wrap, qi + 1, qi), jnp.where(wrap, qi + 1, kb - 1)

    def step(st, new):
        (qa, ka), (qb, kb), (qc, kc) = st
        old = 1 - new
        q0 = pl.multiple_of(jnp.minimum(qa, nq - 1) * blk, blk)
        k0 = pl.multiple_of(jnp.minimum(ka, nq - 1) * blk, blk)
        b0 = pl.multiple_of(jnp.where(ka == qa, 0, blk), blk)
        bias = bias_ref[pl.ds(b0, blk), :]
        z_new = [lax.dot_general(q_ref[pl.ds(q0, blk), hsl[h]],
                                 k_ref[pl.ds(k0, blk), hsl[h]], NT_DIMS,
                                 preferred_element_type=F32) for h in range(heads)]
        first = kb == qb
        af_new = []
        for h in range(heads):
            ls, lk = _sb_logs(z_buf[old, h])
            off = jnp.where(first, 0.0, run_ref[h])
            tw_buf[new, h] = _add_lane_tiled(ls, off)
            af_new.append(_sb_suffix(lk, u_mat))
            run_ref[h] = off + jnp.broadcast_to(jnp.sum(lk, axis=1, keepdims=True), off.shape)
        vc0 = pl.multiple_of(kc * blk, blk)
        qc0 = pl.multiple_of(qc * blk, blk)
        last = kc == 0
        for h in range(heads):
            w = jnp.exp2(tw_buf[old, h] + af_buf[old, h]).astype(BF16)
            acc = acc_ref[h] + jnp.dot(w, v_ref[pl.ds(vc0, blk), hsl[h]],
                                       preferred_element_type=F32)
            gate = za_ref[pl.ds(qc0, blk), hsl[h]].astype(F32)
            o_ref[pl.ds(qc0, blk), hsl[h]] = (acc * gate).astype(o_ref.dtype)
            acc_ref[h] = jnp.where(last, 0.0, acc)
        for h in range(heads):
            z_buf[new, h] = z_new[h] + bias
            af_buf[new, h] = af_new[h]
        return next_pair(qa, ka), (qa, ka), (qb, kb)

    zero = jnp.int32(0)
    n_steps = n_pairs + 2
    st = lax.fori_loop(0, n_steps // 2, lambda _, s: step(step(s, 0), 1),
                       ((zero, zero), (zero, zero), (zero, zero)))
    if n_steps % 2:
        step(st, 0)


def _attn_prompt(q, k, v, za, *, bsz, t_len, dh, heads_per_step, blk):
    width = q.shape[1]
    hw = heads_per_step * dh
    spec = pl.BlockSpec((t_len, hw), lambda b, g: (b, g))
    return pl.pallas_call(
        functools.partial(_attn_prompt_body, heads=heads_per_step, dh=dh, blk=blk),
        grid=(bsz, width // hw),
        in_specs=[spec, spec, spec, spec],
        out_specs=spec,
        out_shape=jax.ShapeDtypeStruct((bsz * t_len, width), BF16),
        scratch_shapes=[pltpu.VMEM((2 * blk, blk), F32),
                        pltpu.VMEM((2, heads_per_step, blk, blk), F32),
                        pltpu.VMEM((2, heads_per_step, blk, blk), F32),
                        pltpu.VMEM((2, heads_per_step, blk, blk), F32),
                        pltpu.VMEM((heads_per_step, blk, LANES), F32),
                        pltpu.VMEM((heads_per_step, blk, dh), F32)],
        compiler_params=_params(2),
        name="attn_prompt",
    )(q, k, v, za)


def _attn_sample_body(q_ref, ck_ref, cv_ref, kn_ref, vn_ref, za_ref, o_ref,
                      *, group, n_heads, dh, chunk):
    t_len = q_ref.shape[0]
    p_len = ck_ref.shape[1] // n_heads
    n_chunks = p_len // chunk
    u_mat = _suffix_ones(chunk)
    u_new = _suffix_ones(t_len)
    bias_new = _causal_bias(t_len)

    def past_rows(ref, head):
        return ref[0, pl.ds(head, p_len, stride=n_heads), :].astype(BF16)

    for g in range(n_heads // group):
        heads = range(g * group, (g + 1) * group)
        hsl = [slice(h * dh, (h + 1) * dh) for h in heads]
        z_past, z_new = [], []
        for h, hs in zip(heads, hsl):
            q = q_ref[:, hs]
            z_past.append(lax.dot_general(q, past_rows(ck_ref, h), NT_DIMS,
                                          preferred_element_type=F32))
            z_new.append(lax.dot_general(q, kn_ref[:, hs], NT_DIMS,
                                         preferred_element_type=F32) + bias_new)
        z_past = jnp.concatenate(z_past, axis=0)
        z_new = jnp.concatenate(z_new, axis=0)

        ls_new, lk_new = _sb_logs(z_new)
        w_new = jnp.exp2(ls_new + _sb_suffix(lk_new, u_new)).astype(BF16)
        off = jnp.sum(lk_new, axis=1, keepdims=True)
        ls_past, lk_past = _sb_logs(z_past)
        w_chunks = [None] * n_chunks
        for c in range(n_chunks - 1, -1, -1):
            cs = slice(c * chunk, (c + 1) * chunk)
            lk = lk_past[:, cs]
            w_chunks[c] = jnp.exp2(ls_past[:, cs] + _sb_suffix(lk, u_mat) + off).astype(BF16)
            off = off + jnp.sum(lk, axis=1, keepdims=True)
        w_past = jnp.concatenate(w_chunks, axis=1)

        for n, (h, hs) in enumerate(zip(heads, hsl)):
            rows = slice(n * t_len, (n + 1) * t_len)
            acc = (jnp.dot(w_past[rows], past_rows(cv_ref, h), preferred_element_type=F32)
                   + jnp.dot(w_new[rows], vn_ref[:, hs], preferred_element_type=F32))
            o_ref[:, hs] = (acc * za_ref[:, hs].astype(F32)).astype(o_ref.dtype)


def _attn_sample(q, ck, cv, kn, vn, za, *, row_offset, t_len, n_heads, dh, group, chunk):
    bsz = ck.shape[0]
    width = q.shape[1]
    r = row_offset // t_len
    shifted = pl.BlockSpec((t_len, width), lambda b: (r + b, 0))
    local = pl.BlockSpec((t_len, width), lambda b: (b, 0))
    past = pl.BlockSpec((1, ck.shape[1], dh), lambda b: (b, 0, 0))
    return pl.pallas_call(
        functools.partial(_attn_sample_body, group=group, n_heads=n_heads, dh=dh, chunk=chunk),
        grid=(bsz,),
        in_specs=[shifted, past, past, local, local, shifted],
        out_specs=local,
        out_shape=jax.ShapeDtypeStruct((bsz * t_len, width), BF16),
        compiler_params=_params(1),
        name="attn_sample",
    )(q, ck, cv, kn, vn, za)


def _conv_body(u_ref, g_ref, st_ref, cw_ref, cb_ref, y_ref, ns_ref):
    u = u_ref[...]
    st = st_ref[0]
    t_len = u.shape[0]
    row = lax.broadcasted_iota(jnp.int32, u.shape, 0)
    u1 = jnp.where(row == 0, st[1:2, :], pltpu.roll(u, 1, axis=0))
    u2 = jnp.where(row == 0, st[0:1, :],
                   jnp.where(row == 1, st[1:2, :], pltpu.roll(u, 2, axis=0)))
    conv = cb_ref[...] + u2 * cw_ref[0:1, :] + u1 * cw_ref[1:2, :] + u * cw_ref[2:3, :]
    y_ref[...] = (g_ref[...].astype(F32) * conv).astype(y_ref.dtype)
    ns_ref[0] = u_ref[t_len - (CONV_WIDTH - 1):, :]


def _conv(u, gate, state, conv_w, conv_b, *, row_offset, t_len, tc):
    bsz = state.shape[0]
    ch = u.shape[1]
    tc = min(tc, ch)
    r = row_offset // t_len
    in_spec = pl.BlockSpec((t_len, tc), lambda b, c: (r + b, c))
    st_spec = pl.BlockSpec((1, CONV_WIDTH - 1, tc), lambda b, c: (b, 0, c))
    return pl.pallas_call(
        _conv_body,
        grid=(bsz, ch // tc),
        in_specs=[in_spec, in_spec, st_spec,
                  pl.BlockSpec((CONV_WIDTH, tc), lambda b, c: (0, c)),
                  pl.BlockSpec((1, tc), lambda b, c: (0, c))],
        out_specs=[pl.BlockSpec((t_len, tc), lambda b, c: (b, c)), st_spec],
        out_shape=[jax.ShapeDtypeStruct((bsz * t_len, ch), BF16),
                   jax.ShapeDtypeStruct((bsz, CONV_WIDTH - 1, ch), F32)],
        compiler_params=_params(2),
        name="short_conv",
    )(u, gate, state, conv_w, conv_b)


def _merge_body(ya_ref, yb_ref, wa_ref, wb_ref, ga_ref, gb_ref, o_ref):
    p_a = jnp.dot(ya_ref[...], wa_ref[...], preferred_element_type=F32)
    p_b = jnp.dot(yb_ref[...], wb_ref[...], preferred_element_type=F32)
    merged = ga_ref[...].astype(F32) * p_a + gb_ref[...].astype(F32) * p_b
    o_ref[...] = merged.astype(o_ref.dtype)


def _merge(y_a, y_b, w_a, w_b, gates, *, row_offset, tm, tn):
    m, ka = y_a.shape
    kb = y_b.shape[1]
    d = w_a.shape[1]
    tn = min(tn, d)
    nj = d // tn
    r = row_offset // tm
    return pl.pallas_call(
        _merge_body,
        grid=(m // tm, nj),
        in_specs=[pl.BlockSpec((tm, ka), lambda i, j: (i, 0)),
                  pl.BlockSpec((tm, kb), lambda i, j: (i, 0)),
                  pl.BlockSpec((ka, tn), lambda i, j: (0, j)),
                  pl.BlockSpec((kb, tn), lambda i, j: (0, j)),
                  pl.BlockSpec((tm, tn), lambda i, j: (r + i, j)),
                  pl.BlockSpec((tm, tn), lambda i, j: (r + i, nj + j))],
        out_specs=pl.BlockSpec((tm, tn), lambda i, j: (i, j)),
        out_shape=jax.ShapeDtypeStruct((m, d), BF16),
        compiler_params=_params(2),
        name="gated_merge",
    )(y_a, y_b, w_a, w_b, gates, gates)


def _out_body(m_ref, w_ref, x_ref, g_ref, b_ref, o_ref, *, alpha):
    pre = alpha * x_ref[...] + jnp.dot(m_ref[...], w_ref[...], preferred_element_type=F32)
    mu = jnp.mean(pre, axis=1, keepdims=True)
    cen = pre - mu
    var = jnp.mean(cen * cen, axis=1, keepdims=True)
    o_ref[...] = cen * lax.rsqrt(var + LN_EPS) * g_ref[...] + b_ref[...]


def _out_proj_ln(merged, w_o, x, ln_g, ln_b, *, alpha, tm):
    m, d = x.shape
    tm = min(tm, m)
    row_spec = lambda: pl.BlockSpec((tm, d), lambda i: (i, 0))
    const_spec = lambda r: pl.BlockSpec((r, d), lambda i: (0, 0), pipeline_mode=pl.Buffered(1))
    return pl.pallas_call(
        functools.partial(_out_body, alpha=alpha),
        grid=(m // tm,),
        in_specs=[row_spec(), const_spec(d), row_spec(), const_spec(1), const_spec(1)],
        out_specs=row_spec(),
        out_shape=jax.ShapeDtypeStruct((m, d), F32),
        compiler_params=_params(1),
        name="out_proj_ln",
    )(merged, w_o, x, ln_g, ln_b)


def _layer(xp, xs, past_k, past_v, past_conv, w_in, b_in, conv_w, conv_b, w_a, w_b, w_o,
           ln_g, ln_b, *, n_heads, dh, alpha):
    (bp, tp, d), (bs, ts, _) = xp.shape, xs.shape
    mp, ms = bp * tp, bs * ts
    w_att = n_heads * dh
    w_conv = conv_w.shape[1]
    tm = math.gcd(mp, ms, ROW_TILE)
    x_all = jnp.concatenate([xp.reshape(mp, d), xs.reshape(ms, d)], axis=0).astype(BF16)
    q_scale = dh ** -0.5 * math.log2(math.e)

    o_q, o_k, o_v, o_za = 0, w_att, 2 * w_att, 3 * w_att
    o_bg = 4 * w_att
    o_cg, o_h, o_zb = o_bg + w_conv, o_bg + 2 * w_conv, o_bg + 3 * w_conv
    o_g = o_bg + 4 * w_conv
    proj = functools.partial(_proj, x_all, w_in, b_in, tm=tm)
    (q,) = proj([o_q], w_att, lambda a: (a * q_scale,), [BF16], tn=512, name="proj_q")
    (za,) = proj([o_za], w_att, lambda a: (_silu(a),), [BF16], tn=512, name="proj_za")
    (gate_b,) = proj([o_bg, o_zb], w_conv, lambda bg, zb: (_silu(zb) * bg,), [BF16], tn=256,
                     name="proj_gate_b")
    (u,) = proj([o_cg, o_h], w_conv, lambda cg, h: (cg * h,), [F32], tn=256, name="proj_u")
    (gates,) = proj([o_g], 2 * d, lambda a: (_sigmoid(a),), [BF16], tn=512, name="proj_gates")

    results = []
    for group_rows, row_offset, bsz, t_len, x in ((mp, 0, bp, tp, xp), (ms, mp, bs, ts, xs)):
        heads_proj = functools.partial(_proj_heads, x_all, w_in, b_in, width=w_att, dh=dh,
                                       row_offset=row_offset, rows=group_rows,
                                       tm=max(tm // 2, 8), tn=1024)
        k, kb16 = heads_proj(col_offset=o_k, name="proj_k")
        v, vb16 = heads_proj(col_offset=o_v, name="proj_v")
        if row_offset == 0:
            y_a = _attn_prompt(q, kb16, vb16, za, bsz=bsz, t_len=t_len, dh=dh, heads_per_step=2,
                               blk=ATT_BLOCK)
            state = jnp.zeros((bsz, CONV_WIDTH - 1, w_conv), F32)
            tc = 512
        else:
            p_len = past_k.shape[1]
            y_a = _attn_sample(q, past_k.reshape(bsz, p_len * n_heads, dh),
                               past_v.reshape(bsz, p_len * n_heads, dh), kb16, vb16, za,
                               row_offset=row_offset, t_len=t_len, n_heads=n_heads, dh=dh,
                               group=4, chunk=ATT_BLOCK)
            state = past_conv
            tc = w_conv
        y_b, new_conv = _conv(u, gate_b, state, conv_w, conv_b, row_offset=row_offset,
                              t_len=t_len, tc=tc)
        merged = _merge(y_a, y_b, w_a, w_b, gates, row_offset=row_offset, tm=tm, tn=1024)
        y = _out_proj_ln(merged, w_o, x.reshape(group_rows, d), ln_g, ln_b, alpha=alpha, tm=128)
        results.append((y.reshape(bsz, t_len, d), k.reshape(bsz, t_len, n_heads, dh),
                        v.reshape(bsz, t_len, n_heads, dh), new_conv))
    return results


def kernel(x_prompt, x_sample, cache_k, cache_v, state_conv, w_in, b_in, conv_w, conv_b,
           w_a, w_b, w_o, ln_g, ln_b):
    depth = w_in.shape[0]
    n_heads, dh = cache_k.shape[-2:]
    alpha = (2.0 * depth) ** 0.25
    xp, xs = x_prompt, x_sample
    kp, vp, cp, kn, vn, cn = [], [], [], [], [], []
    for l in range(depth):
        (xp, k1, v1, c1), (xs, k2, v2, c2) = _layer(
            xp, xs, cache_k[l], cache_v[l], state_conv[l],
            w_in[l], b_in[l][None, :], conv_w[l], conv_b[l][None, :],
            w_a[l].astype(BF16), w_b[l].astype(BF16), w_o[l].astype(BF16),
            ln_g[l][None, :], ln_b[l][None, :], n_heads=n_heads, dh=dh, alpha=alpha)
        kp.append(k1); vp.append(v1); cp.append(c1)
        kn.append(k2); vn.append(v2); cn.append(c2)
    return (xp, xs, jnp.stack(kp), jnp.stack(vp), jnp.stack(cp),
            jnp.stack(kn), jnp.stack(vn), jnp.stack(cn))
```

```python
import functools
import math

import jax
import jax.numpy as jnp
from jax import lax
from jax.experimental import pallas as pl
from jax.experimental.pallas import tpu as pltpu

F32 = jnp.float32
BF16 = jnp.bfloat16

LN_EPS = 1e-5
CONV_WIDTH = 3
VMEM_LIMIT_BYTES = 56 * 1024 * 1024
OUT_PROJ_VMEM_LIMIT_BYTES = 60 * 1024 * 1024
LANES = 128
ATT_BLOCK = 256
ROW_TILE = 1024
MASKED_SCORE = -1e30
NT_DIMS = (((1,), (1,)), ((), ()))


def _params(n_grid_dims, vmem_limit_bytes=VMEM_LIMIT_BYTES):
    return pltpu.CompilerParams(
        dimension_semantics=("arbitrary",) * n_grid_dims,
        vmem_limit_bytes=vmem_limit_bytes,
    )


def _sigmoid(x):
    return 1.0 / (1.0 + jnp.exp(-x))


def _silu(x):
    return x * _sigmoid(x)


def _proj_body(*refs, n_slabs, epilogue):
    x_ref = refs[0]
    w_refs = refs[1:1 + n_slabs]
    b_refs = refs[1 + n_slabs:1 + 2 * n_slabs]
    o_refs = refs[1 + 2 * n_slabs:]
    x = x_ref[...]
    accs = [jnp.dot(x, w[...], preferred_element_type=F32) + b[...]
            for w, b in zip(w_refs, b_refs)]
    for o_ref, val in zip(o_refs, epilogue(*accs)):
        o_ref[...] = val.astype(o_ref.dtype)


def _proj(x, w, b, col_offsets, width, epilogue, out_dtypes, *, tm, tn, name):
    m, k = x.shape
    n_slabs = len(col_offsets)
    tn = min(tn, width)
    in_specs = [pl.BlockSpec((tm, k), lambda i, j: (i, 0))]
    for off in col_offsets:
        in_specs.append(pl.BlockSpec((k, tn), lambda i, j, o=off // tn: (0, o + j)))
    for off in col_offsets:
        in_specs.append(pl.BlockSpec((1, tn), lambda i, j, o=off // tn: (0, o + j)))
    return pl.pallas_call(
        functools.partial(_proj_body, n_slabs=n_slabs, epilogue=epilogue),
        grid=(m // tm, width // tn),
        in_specs=in_specs,
        out_specs=[pl.BlockSpec((tm, tn), lambda i, j: (i, j)) for _ in out_dtypes],
        out_shape=[jax.ShapeDtypeStruct((m, width), dt) for dt in out_dtypes],
        compiler_params=_params(2),
        name=name,
    )(x, *([w] * n_slabs), *([b] * n_slabs))


def _proj_heads_body(x_ref, w_ref, b_ref, of_ref, ob_ref):
    acc = jnp.dot(x_ref[...], w_ref[...], preferred_element_type=F32) + b_ref[...]
    ob_ref[...] = acc.astype(ob_ref.dtype)
    _, heads, dh = of_ref.shape
    for h in range(heads):
        of_ref[:, h, :] = acc[:, h * dh:(h + 1) * dh]


def _proj_heads(x, w, b, col_offset, width, dh, *, tm, tn, name):
    m, k = x.shape
    tn = min(tn, width)
    o = col_offset // tn
    return pl.pallas_call(
        _proj_heads_body,
        grid=(m // tm, width // tn),
        in_specs=[pl.BlockSpec((tm, k), lambda i, j: (i, 0)),
                  pl.BlockSpec((k, tn), lambda i, j: (0, o + j)),
                  pl.BlockSpec((1, tn), lambda i, j: (0, o + j))],
        out_specs=[pl.BlockSpec((tm, tn // dh, dh), lambda i, j: (i, j, 0)),
                   pl.BlockSpec((tm, tn), lambda i, j: (i, j))],
        out_shape=[jax.ShapeDtypeStruct((m, width // dh, dh), F32),
                   jax.ShapeDtypeStruct((m, width), BF16)],
        compiler_params=_params(2),
        name=name,
    )(x, w, b)


def _suffix_ones(n):
    j = lax.broadcasted_iota(jnp.int32, (n, n), 0)
    s = lax.broadcasted_iota(jnp.int32, (n, n), 1)
    return (j > s).astype(BF16)


def _causal_bias(n):
    row = lax.broadcasted_iota(jnp.int32, (n, n), 0)
    col = lax.broadcasted_iota(jnp.int32, (n, n), 1)
    return jnp.where(col < row, 0.0, MASKED_SCORE).astype(F32)


def _sb_logs(z):
    ls = jnp.minimum(z, 0.0) - jnp.log2(1.0 + jnp.exp2(-jnp.abs(z)))
    return ls, ls - z


def _add_lane_tiled(a, r):
    return jnp.concatenate(
        [a[:, c:c + LANES] + r for c in range(0, a.shape[1], LANES)], axis=1)


def _sb_suffix(lk, u_mat):
    return jnp.dot(lk.astype(BF16), u_mat, preferred_element_type=F32)


def _attn_prompt_body(q_ref, k_ref, v_ref, za_ref, o_ref,
                      bias_ref, z_buf, tw_buf, af_buf, run_ref, acc_ref, *, heads, dh, blk):
    t_len = q_ref.shape[0]
    nq = t_len // blk
    n_pairs = nq * (nq + 1) // 2
    u_mat = _suffix_ones(blk)
    bias_ref[0:blk, :] = _causal_bias(blk)
    bias_ref[blk:2 * blk, :] = jnp.zeros((blk, blk), F32)
    hsl = [slice(h * dh, (h + 1) * dh) for h in range(heads)]
    for h in range(heads):
        z_buf[1, h] = jnp.full((blk, blk), MASKED_SCORE, F32)
        tw_buf[1, h] = jnp.full((blk, blk), MASKED_SCORE, F32)
        af_buf[1, h] = jnp.zeros((blk, blk), F32)
        run_ref[h] = jnp.zeros((blk, LANES), F32)
        acc_ref[h] = jnp.zeros((blk, dh), F32)

    def next_pair(qi, kb):
        wrap = kb == 0
        return jnp.where(wrap, qi + 1, qi), jnp.where(wrap, qi + 1, kb - 1)

    def step(st, new):
        (qa, ka), (qb, kb), (qc, kc) = st
        old = 1 - new
        q0 = pl.multiple_of(jnp.minimum(qa, nq - 1) * blk, blk)
        k0 = pl.multiple_of(jnp.minimum(ka, nq - 1) * blk, blk)
        b0 = pl.multiple_of(jnp.where(ka == qa, 0, blk), blk)
        bias = bias_ref[pl.ds(b0, blk), :]
        z_new = [lax.dot_general(q_ref[pl.ds(q0, blk), hsl[h]],
                                 k_ref[pl.ds(k0, blk), hsl[h]], NT_DIMS,
                                 preferred_element_type=F32) for h in range(heads)]
        first = kb == qb
        af_new = []
        for h in range(heads):
            ls, lk = _sb_logs(z_buf[old, h])
            off = jnp.where(first, 0.0, run_ref[h])
            tw_buf[new, h] = _add_lane_tiled(ls, off)
            af_new.append(_sb_suffix(lk, u_mat))
            run_ref[h] = off + jnp.broadcast_to(jnp.sum(lk, axis=1, keepdims=True), off.shape)
        vc0 = pl.multiple_of(kc * blk, blk)
        qc0 = pl.multiple_of(qc * blk, blk)
        last = kc == 0
        for h in range(heads):
            w = jnp.exp2(tw_buf[old, h] + af_buf[old, h]).astype(BF16)
            acc = acc_ref[h] + jnp.dot(w, v_ref[pl.ds(vc0, blk), hsl[h]],
                                       preferred_element_type=F32)
            gate = za_ref[pl.ds(qc0, blk), hsl[h]].astype(F32)
            o_ref[pl.ds(qc0, blk), hsl[h]] = (acc * gate).astype(o_ref.dtype)
            acc_ref[h] = jnp.where(last, 0.0, acc)
        for h in range(heads):
            z_buf[new, h] = z_new[h] + bias
            af_buf[new, h] = af_new[h]
        return next_pair(qa, ka), (qa, ka), (qb, kb)

    zero = jnp.int32(0)
    n_steps = n_pairs + 2
    st = lax.fori_loop(0, n_steps // 2, lambda _, s: step(step(s, 0), 1),
                       ((zero, zero), (zero, zero), (zero, zero)))
    if n_steps % 2:
        step(st, 0)


def _attn_prompt(q, k, v, za, *, bsz, t_len, dh, heads_per_step, blk):
    width = q.shape[1]
    hw = heads_per_step * dh
    spec = pl.BlockSpec((t_len, hw), lambda b, g: (b, g))
    return pl.pallas_call(
        functools.partial(_attn_prompt_body, heads=heads_per_step, dh=dh, blk=blk),
        grid=(bsz, width // hw),
        in_specs=[spec, spec, spec, spec],
        out_specs=spec,
        out_shape=jax.ShapeDtypeStruct((bsz * t_len, width), BF16),
        scratch_shapes=[pltpu.VMEM((2 * blk, blk), F32),
                        pltpu.VMEM((2, heads_per_step, blk, blk), F32),
                        pltpu.VMEM((2, heads_per_step, blk, blk), F32),
                        pltpu.VMEM((2, heads_per_step, blk, blk), F32),
                        pltpu.VMEM((heads_per_step, blk, LANES), F32),
                        pltpu.VMEM((heads_per_step, blk, dh), F32)],
        compiler_params=_params(2),
        name="attn_prompt",
    )(q, k, v, za)


def _attn_sample_body(q_ref, ck_hbm, cv_hbm, kn_ref, vn_ref, za_ref, o_ref, k_buf, v_buf, sems,
                      *, group, chunk):
    b = pl.program_id(0)
    n_streams = pl.num_programs(0)
    t_len = q_ref.shape[0]
    _, n_heads, p_len, dh = k_buf.shape
    n_chunks = p_len // chunk
    u_mat = _suffix_ones(chunk)
    u_new = _suffix_ones(t_len)
    bias_new = _causal_bias(t_len)
    slot = lax.rem(b, 2)

    def head_copies(stream, s):
        return [pltpu.make_async_copy(src.at[stream, :, h, :], dst.at[s, h], sems.at[s, a, h])
                for a, (src, dst) in enumerate(((ck_hbm, k_buf), (cv_hbm, v_buf)))
                for h in range(n_heads)]

    @pl.when(b == 0)
    def _():
        for cp in head_copies(0, 0):
            cp.start()

    @pl.when(b + 1 < n_streams)
    def _():
        for cp in head_copies(b + 1, 1 - slot):
            cp.start()

    for cp in head_copies(b, slot):
        cp.wait()

    def past_rows(buf, head):
        return buf[slot, head].astype(BF16)

    ck_ref, cv_ref = k_buf, v_buf
    for g in range(n_heads // group):
        heads = range(g * group, (g + 1) * group)
        hsl = [slice(h * dh, (h + 1) * dh) for h in heads]
        z_past, z_new = [], []
        for h, hs in zip(heads, hsl):
            q = q_ref[:, hs]
            z_past.append(lax.dot_general(q, past_rows(ck_ref, h), NT_DIMS,
                                          preferred_element_type=F32))
            z_new.append(lax.dot_general(q, kn_ref[:, hs], NT_DIMS,
                                         preferred_element_type=F32) + bias_new)
        z_past = jnp.concatenate(z_past, axis=0)
        z_new = jnp.concatenate(z_new, axis=0)

        ls_new, lk_new = _sb_logs(z_new)
        w_new = jnp.exp2(ls_new + _sb_suffix(lk_new, u_new)).astype(BF16)
        off = jnp.sum(lk_new, axis=1, keepdims=True)
        ls_past, lk_past = _sb_logs(z_past)
        w_chunks = [None] * n_chunks
        for c in range(n_chunks - 1, -1, -1):
            cs = slice(c * chunk, (c + 1) * chunk)
            lk = lk_past[:, cs]
            w_chunks[c] = jnp.exp2(ls_past[:, cs] + _sb_suffix(lk, u_mat) + off).astype(BF16)
            off = off + jnp.sum(lk, axis=1, keepdims=True)
        w_past = jnp.concatenate(w_chunks, axis=1)

        for n, (h, hs) in enumerate(zip(heads, hsl)):
            rows = slice(n * t_len, (n + 1) * t_len)
            acc = (jnp.dot(w_past[rows], past_rows(cv_ref, h), preferred_element_type=F32)
                   + jnp.dot(w_new[rows], vn_ref[:, hs], preferred_element_type=F32))
            o_ref[:, hs] = (acc * za_ref[:, hs].astype(F32)).astype(o_ref.dtype)


def _attn_sample(q, ck, cv, kn, vn, za, *, row_offset, t_len, group, chunk):
    bsz, p_len, n_heads, dh = ck.shape
    width = q.shape[1]
    r = row_offset // t_len
    shifted = pl.BlockSpec((t_len, width), lambda b: (r + b, 0))
    local = pl.BlockSpec((t_len, width), lambda b: (b, 0))
    in_hbm = pl.BlockSpec(memory_space=pltpu.HBM)
    return pl.pallas_call(
        functools.partial(_attn_sample_body, group=group, chunk=chunk),
        grid=(bsz,),
        in_specs=[shifted, in_hbm, in_hbm, local, local, shifted],
        out_specs=local,
        out_shape=jax.ShapeDtypeStruct((bsz * t_len, width), BF16),
        scratch_shapes=[pltpu.VMEM((2, n_heads, p_len, dh), F32),
                        pltpu.VMEM((2, n_heads, p_len, dh), F32),
                        pltpu.SemaphoreType.DMA((2, 2, n_heads))],
        compiler_params=_params(1),
        name="attn_sample",
    )(q, ck, cv, kn, vn, za)


def _conv_body(u_ref, g_ref, st_ref, cw_ref, cb_ref, y_ref, ns_ref):
    u = u_ref[...]
    st = st_ref[0]
    t_len = u.shape[0]
    row = lax.broadcasted_iota(jnp.int32, u.shape, 0)
    u1 = jnp.where(row == 0, st[1:2, :], pltpu.roll(u, 1, axis=0))
    u2 = jnp.where(row == 0, st[0:1, :],
                   jnp.where(row == 1, st[1:2, :], pltpu.roll(u, 2, axis=0)))
    conv = cb_ref[...] + u2 * cw_ref[0:1, :] + u1 * cw_ref[1:2, :] + u * cw_ref[2:3, :]
    y_ref[...] = (g_ref[...].astype(F32) * conv).astype(y_ref.dtype)
    ns_ref[0] = u_ref[t_len - (CONV_WIDTH - 1):, :]


def _conv(u, gate, state, conv_w, conv_b, *, row_offset, t_len, tc):
    bsz = state.shape[0]
    ch = u.shape[1]
    tc = min(tc, ch)
    r = row_offset // t_len
    in_spec = pl.BlockSpec((t_len, tc), lambda b, c: (r + b, c))
    st_spec = pl.BlockSpec((1, CONV_WIDTH - 1, tc), lambda b, c: (b, 0, c))
    return pl.pallas_call(
        _conv_body,
        grid=(bsz, ch // tc),
        in_specs=[in_spec, in_spec, st_spec,
                  pl.BlockSpec((CONV_WIDTH, tc), lambda b, c: (0, c)),
                  pl.BlockSpec((1, tc), lambda b, c: (0, c))],
        out_specs=[pl.BlockSpec((t_len, tc), lambda b, c: (b, c)), st_spec],
        out_shape=[jax.ShapeDtypeStruct((bsz * t_len, ch), BF16),
                   jax.ShapeDtypeStruct((bsz, CONV_WIDTH - 1, ch), F32)],
        compiler_params=_params(2),
        name="short_conv",
    )(u, gate, state, conv_w, conv_b)


def _merge_body(ya_ref, yb_ref, wa_ref, wb_ref, ga_ref, gb_ref, o_ref):
    p_a = jnp.dot(ya_ref[...], wa_ref[...], preferred_element_type=F32)
    p_b = jnp.dot(yb_ref[...], wb_ref[...], preferred_element_type=F32)
    merged = ga_ref[...].astype(F32) * p_a + gb_ref[...].astype(F32) * p_b
    o_ref[...] = merged.astype(o_ref.dtype)


def _merge(y_a, y_b, w_a, w_b, gates, *, row_offset, tm, tn):
    m, ka = y_a.shape
    kb = y_b.shape[1]
    d = w_a.shape[1]
    tn = min(tn, d)
    nj = d // tn
    r = row_offset // tm
    return pl.pallas_call(
        _merge_body,
        grid=(m // tm, nj),
        in_specs=[pl.BlockSpec((tm, ka), lambda i, j: (i, 0)),
                  pl.BlockSpec((tm, kb), lambda i, j: (i, 0)),
                  pl.BlockSpec((ka, tn), lambda i, j: (0, j)),
                  pl.BlockSpec((kb, tn), lambda i, j: (0, j)),
                  pl.BlockSpec((tm, tn), lambda i, j: (r + i, j)),
                  pl.BlockSpec((tm, tn), lambda i, j: (r + i, nj + j))],
        out_specs=pl.BlockSpec((tm, tn), lambda i, j: (i, j)),
        out_shape=jax.ShapeDtypeStruct((m, d), BF16),
        compiler_params=_params(2),
        name="gated_merge",
    )(y_a, y_b, w_a, w_b, gates, gates)


def _out_body(m_ref, w_ref, x_ref, g_ref, b_ref, o_ref, *, alpha):
    pre = alpha * x_ref[...] + jnp.dot(m_ref[...], w_ref[...], preferred_element_type=F32)
    mu = jnp.mean(pre, axis=1, keepdims=True)
    cen = pre - mu
    var = jnp.mean(cen * cen, axis=1, keepdims=True)
    o_ref[...] = cen * lax.rsqrt(var + LN_EPS) * g_ref[...] + b_ref[...]


def _out_proj_ln(merged, w_o, x, ln_g, ln_b, *, alpha, tm):
    m, d = x.shape
    tm = min(tm, m)
    row_spec = lambda: pl.BlockSpec((tm, d), lambda i: (i, 0))
    const_spec = lambda r: pl.BlockSpec((r, d), lambda i: (0, 0), pipeline_mode=pl.Buffered(1))
    return pl.pallas_call(
        functools.partial(_out_body, alpha=alpha),
        grid=(m // tm,),
        in_specs=[row_spec(), const_spec(d), row_spec(), const_spec(1), const_spec(1)],
        out_specs=row_spec(),
        out_shape=jax.ShapeDtypeStruct((m, d), F32),
        compiler_params=_params(1, vmem_limit_bytes=OUT_PROJ_VMEM_LIMIT_BYTES),
        name="out_proj_ln",
    )(merged, w_o, x, ln_g, ln_b)


def _layer(x, past_k, past_v, past_conv, w_in, b_in, conv_w, conv_b, w_a, w_b, w_o, ln_g, ln_b,
           *, n_heads, dh, alpha):
    bsz, t_len, d = x.shape
    m = bsz * t_len
    w_att = n_heads * dh
    w_conv = conv_w.shape[1]
    tm = min(ROW_TILE, m)
    xb = x.reshape(m, d).astype(BF16)
    q_scale = dh ** -0.5 * math.log2(math.e)

    o_q, o_k, o_v, o_za = 0, w_att, 2 * w_att, 3 * w_att
    o_bg = 4 * w_att
    o_cg, o_h, o_zb = o_bg + w_conv, o_bg + 2 * w_conv, o_bg + 3 * w_conv
    o_g = o_bg + 4 * w_conv
    proj = functools.partial(_proj, xb, w_in, b_in, tm=tm)
    (q,) = proj([o_q], w_att, lambda a: (a * q_scale,), [BF16], tn=1024, name="proj_q")
    k, kb16 = _proj_heads(xb, w_in, b_in, o_k, w_att, dh, tm=tm, tn=1024, name="proj_k")
    v, vb16 = _proj_heads(xb, w_in, b_in, o_v, w_att, dh, tm=tm, tn=1024, name="proj_v")
    (za,) = proj([o_za], w_att, lambda a: (_silu(a),), [BF16], tn=1024, name="proj_za")
    (gate_b,) = proj([o_bg, o_zb], w_conv, lambda bg, zb: (_silu(zb) * bg,), [BF16], tn=512,
                     name="proj_gate_b")
    (u,) = proj([o_cg, o_h], w_conv, lambda cg, h: (cg * h,), [F32], tn=512, name="proj_u")
    (gates,) = proj([o_g], 2 * d, lambda a: (_sigmoid(a),), [BF16], tn=1024, name="proj_gates")

    if past_k is None:
        y_a = _attn_prompt(q, kb16, vb16, za, bsz=bsz, t_len=t_len, dh=dh, heads_per_step=2,
                           blk=ATT_BLOCK)
        past_conv = jnp.zeros((bsz, CONV_WIDTH - 1, w_conv), F32)
        tc = 512
    else:
        y_a = _attn_sample(q, past_k, past_v, kb16, vb16, za, row_offset=0, t_len=t_len,
                           group=4, chunk=ATT_BLOCK)
        tc = w_conv
    y_b, new_conv = _conv(u, gate_b, past_conv, conv_w, conv_b, row_offset=0, t_len=t_len, tc=tc)
    merged = _merge(y_a, y_b, w_a, w_b, gates, row_offset=0, tm=tm, tn=1024)
    y = _out_proj_ln(merged, w_o, x.reshape(m, d), ln_g, ln_b, alpha=alpha, tm=256)
    return (y.reshape(bsz, t_len, d), k.reshape(bsz, t_len, n_heads, dh),
            v.reshape(bsz, t_len, n_heads, dh), new_conv)


def kernel(x_prompt, x_sample, cache_k, cache_v, state_conv, w_in, b_in, conv_w, conv_b,
           w_a, w_b, w_o, ln_g, ln_b):
    depth = w_in.shape[0]
    n_heads, dh = cache_k.shape[-2:]
    alpha = (2.0 * depth) ** 0.25
    xp, xs = x_prompt, x_sample
    kp, vp, cp, kn, vn, cn = [], [], [], [], [], []
    for l in range(depth):
        wts = (w_in[l].astype(BF16), b_in[l][None, :], conv_w[l], conv_b[l][None, :],
               w_a[l].astype(BF16), w_b[l].astype(BF16), w_o[l].astype(BF16),
               ln_g[l][None, :], ln_b[l][None, :])
        layer = functools.partial(_layer, n_heads=n_heads, dh=dh, alpha=alpha)
        xp, k1, v1, c1 = layer(xp, None, None, None, *wts)
        xs, k2, v2, c2 = layer(xs, cache_k[l], cache_v[l], state_conv[l], *wts)
        kp.append(k1); vp.append(v1); cp.append(c1)
        kn.append(k2); vn.append(v2); cn.append(c2)
    return (xp, xs, jnp.stack(kp), jnp.stack(vp), jnp.stack(cp),
            jnp.stack(kn), jnp.stack(vn), jnp.stack(cn))
```

```python
import functools
import math

import jax
import jax.numpy as jnp
from jax import lax
from jax.experimental import pallas as pl
from jax.experimental.pallas import tpu as pltpu

F32 = jnp.float32
BF16 = jnp.bfloat16

LN_EPS = 1e-5
CONV_WIDTH = 3
VMEM_LIMIT_BYTES = 58 * 1024 * 1024
OUT_PROJ_VMEM_LIMIT_BYTES = 60 * 1024 * 1024
LANES = 128
ATT_BLOCK = 256
ROW_TILE = 1024
MASKED_SCORE = -1e30
NT_DIMS = (((1,), (1,)), ((), ()))


def _params(n_grid_dims, vmem_limit_bytes=VMEM_LIMIT_BYTES):
    return pltpu.CompilerParams(
        dimension_semantics=("arbitrary",) * n_grid_dims,
        vmem_limit_bytes=vmem_limit_bytes,
    )


def _sigmoid(x):
    return 1.0 / (1.0 + jnp.exp(-x))


def _silu(x):
    return x * _sigmoid(x)


def _proj_body(*refs, n_slabs, n_out, epilogue):
    x_ref = refs[0]
    w_refs = refs[1:1 + n_slabs]
    b_refs = refs[1 + n_slabs:1 + 2 * n_slabs]
    o_refs = refs[-n_out:]
    x = x_ref[...]
    accs = [jnp.dot(x, w[...], preferred_element_type=F32) + b[...]
            for w, b in zip(w_refs, b_refs)]
    for o_ref, val in zip(o_refs, epilogue(*accs)):
        o_ref[...] = val.astype(o_ref.dtype)
    if len(refs) > 1 + 2 * n_slabs + n_out:
        refs[1 + 2 * n_slabs + 1][...] = refs[1 + 2 * n_slabs][...].astype(BF16)


def _proj(x, w, b, col_offsets, width, epilogue, out_dtypes, *, tm, tn, name,
          round_cols=None):
    m, k = x.shape
    n_slabs = len(col_offsets)
    tn = min(tn, width)
    ni, nj = m // tm, width // tn
    in_specs = [pl.BlockSpec((tm, k), lambda i, j: (i, 0))]
    for off in col_offsets:
        in_specs.append(pl.BlockSpec((k, tn), lambda i, j, o=off // tn: (0, o + j)))
    for off in col_offsets:
        in_specs.append(pl.BlockSpec((1, tn), lambda i, j, o=off // tn: (0, o + j)))
    out_specs = [pl.BlockSpec((tm, tn), lambda i, j: (i, j)) for _ in out_dtypes]
    out_shape = [jax.ShapeDtypeStruct((m, width), dt) for dt in out_dtypes]
    operands = [x, *([w] * n_slabs), *([b] * n_slabs)]
    if round_cols is not None:
        src, n_cols = round_cols
        chunk, rem = divmod(n_cols, ni * nj)
        assert rem == 0 and chunk % LANES == 0, (n_cols, ni, nj)
        chunk_spec = pl.BlockSpec((src.shape[0], chunk), lambda i, j: (0, i * nj + j))
        in_specs.append(chunk_spec)
        out_specs.insert(0, chunk_spec)
        out_shape.insert(0, jax.ShapeDtypeStruct((src.shape[0], n_cols), BF16))
        operands.append(src)
    return pl.pallas_call(
        functools.partial(_proj_body, n_slabs=n_slabs, n_out=len(out_dtypes), epilogue=epilogue),
        grid=(ni, nj),
        in_specs=in_specs,
        out_specs=out_specs,
        out_shape=out_shape,
        compiler_params=_params(2),
        name=name,
    )(*operands)


def _proj_heads_body(x_ref, w_ref, b_ref, of_ref, ob_ref):
    acc = jnp.dot(x_ref[...], w_ref[...], preferred_element_type=F32) + b_ref[...]
    ob_ref[...] = acc.astype(ob_ref.dtype)
    _, heads, dh = of_ref.shape
    for h in range(heads):
        of_ref[:, h, :] = acc[:, h * dh:(h + 1) * dh]


def _proj_heads(x, w, b, col_offset, width, dh, *, tm, tn, name):
    m, k = x.shape
    tn = min(tn, width)
    o = col_offset // tn
    return pl.pallas_call(
        _proj_heads_body,
        grid=(m // tm, width // tn),
        in_specs=[pl.BlockSpec((tm, k), lambda i, j: (i, 0)),
                  pl.BlockSpec((k, tn), lambda i, j: (0, o + j)),
                  pl.BlockSpec((1, tn), lambda i, j: (0, o + j))],
        out_specs=[pl.BlockSpec((tm, tn // dh, dh), lambda i, j: (i, j, 0)),
                   pl.BlockSpec((tm, tn), lambda i, j: (i, j))],
        out_shape=[jax.ShapeDtypeStruct((m, width // dh, dh), F32),
                   jax.ShapeDtypeStruct((m, width), BF16)],
        compiler_params=_params(2),
        name=name,
    )(x, w, b)


def _suffix_ones(n):
    j = lax.broadcasted_iota(jnp.int32, (n, n), 0)
    s = lax.broadcasted_iota(jnp.int32, (n, n), 1)
    return (j > s).astype(BF16)


def _causal_bias(n):
    row = lax.broadcasted_iota(jnp.int32, (n, n), 0)
    col = lax.broadcasted_iota(jnp.int32, (n, n), 1)
    return jnp.where(col < row, 0.0, MASKED_SCORE).astype(F32)


def _sb_logs(z):
    ls = jnp.minimum(z, 0.0) - jnp.log2(1.0 + jnp.exp2(-jnp.abs(z)))
    return ls, ls - z


def _add_lane_tiled(a, r):
    return jnp.concatenate(
        [a[:, c:c + LANES] + r for c in range(0, a.shape[1], LANES)], axis=1)


def _sb_suffix(lk, u_mat):
    return jnp.dot(lk.astype(BF16), u_mat, preferred_element_type=F32)


def _attn_prompt_body(q_ref, k_ref, v_ref, za_ref, o_ref,
                      bias_ref, z_buf, tw_buf, af_buf, run_ref, acc_ref, *, heads, dh, blk):
    t_len = q_ref.shape[0]
    nq = t_len // blk
    n_pairs = nq * (nq + 1) // 2
    u_mat = _suffix_ones(blk)
    bias_ref[0:blk, :] = _causal_bias(blk)
    bias_ref[blk:2 * blk, :] = jnp.zeros((blk, blk), F32)
    hsl = [slice(h * dh, (h + 1) * dh) for h in range(heads)]
    for h in range(heads):
        z_buf[1, h] = jnp.full((blk, blk), MASKED_SCORE, F32)
        tw_buf[1, h] = jnp.full((blk, blk), MASKED_SCORE, F32)
        af_buf[1, h] = jnp.zeros((blk, blk), F32)
        run_ref[h] = jnp.zeros((blk, LANES), F32)
        acc_ref[h] = jnp.zeros((blk, dh), F32)

    def next_pair(qi, kb):
        wrap = kb == 0
        return jnp.where(wrap, qi + 1, qi), jnp.where(wrap, qi + 1, kb - 1)

    def step(st, new):
        (qa, ka), (qb, kb), (qc, kc) = st
        old = 1 - new
        q0 = pl.multiple_of(jnp.minimum(qa, nq - 1) * blk, blk)
        k0 = pl.multiple_of(jnp.minimum(ka, nq - 1) * blk, blk)
        b0 = pl.multiple_of(jnp.where(ka == qa, 0, blk), blk)
        bias = bias_ref[pl.ds(b0, blk), :]
        z_new = [lax.dot_general(q_ref[pl.ds(q0, blk), hsl[h]],
                                 k_ref[pl.ds(k0, blk), hsl[h]], NT_DIMS,
                                 preferred_element_type=F32) for h in range(heads)]
        first = kb == qb
        af_new = []
        for h in range(heads):
            ls, lk = _sb_logs(z_buf[old, h])
            off = jnp.where(first, 0.0, run_ref[h])
            tw_buf[new, h] = _add_lane_tiled(ls, off)
            af_new.append(_sb_suffix(lk, u_mat))
            run_ref[h] = off + jnp.broadcast_to(jnp.sum(lk, axis=1, keepdims=True), off.shape)
        vc0 = pl.multiple_of(kc * blk, blk)
        qc0 = pl.multiple_of(qc * blk, blk)
        last = kc == 0
        for h in range(heads):
            w = jnp.exp2(tw_buf[old, h] + af_buf[old, h]).astype(BF16)
            acc = acc_ref[h] + jnp.dot(w, v_ref[pl.ds(vc0, blk), hsl[h]],
                                       preferred_element_type=F32)
            gate = za_ref[pl.ds(qc0, blk), hsl[h]].astype(F32)
            o_ref[pl.ds(qc0, blk), hsl[h]] = (acc * gate).astype(o_ref.dtype)
            acc_ref[h] = jnp.where(last, 0.0, acc)
        for h in range(heads):
            z_buf[new, h] = z_new[h] + bias
            af_buf[new, h] = af_new[h]
        return next_pair(qa, ka), (qa, ka), (qb, kb)

    zero = jnp.int32(0)
    n_steps = n_pairs + 2
    st = lax.fori_loop(0, n_steps // 2, lambda _, s: step(step(s, 0), 1),
                       ((zero, zero), (zero, zero), (zero, zero)))
    if n_steps % 2:
        step(st, 0)


def _attn_prompt(q, k, v, za, *, bsz, t_len, dh, heads_per_step, blk):
    width = q.shape[1]
    hw = heads_per_step * dh
    spec = pl.BlockSpec((t_len, hw), lambda b, g: (b, g))
    return pl.pallas_call(
        functools.partial(_attn_prompt_body, heads=heads_per_step, dh=dh, blk=blk),
        grid=(bsz, width // hw),
        in_specs=[spec, spec, spec, spec],
        out_specs=spec,
        out_shape=jax.ShapeDtypeStruct((bsz * t_len, width), BF16),
        scratch_shapes=[pltpu.VMEM((2 * blk, blk), F32),
                        pltpu.VMEM((2, heads_per_step, blk, blk), F32),
                        pltpu.VMEM((2, heads_per_step, blk, blk), F32),
                        pltpu.VMEM((2, heads_per_step, blk, blk), F32),
                        pltpu.VMEM((heads_per_step, blk, LANES), F32),
                        pltpu.VMEM((heads_per_step, blk, dh), F32)],
        compiler_params=_params(2),
        name="attn_prompt",
    )(q, k, v, za)


def _attn_sample_body(q_ref, ck_hbm, cv_hbm, kn_ref, vn_ref, za_ref, o_ref, k_buf, v_buf, sems,
                      *, group, chunk):
    b = pl.program_id(0)
    n_streams = pl.num_programs(0)
    t_len = q_ref.shape[0]
    _, n_heads, p_len, dh = k_buf.shape
    n_chunks = p_len // chunk
    u_mat = _suffix_ones(chunk)
    u_new = _suffix_ones(t_len)
    bias_new = _causal_bias(t_len)
    slot = lax.rem(b, 2)

    def head_copies(stream, s):
        return [pltpu.make_async_copy(src.at[stream, :, h, :], dst.at[s, h], sems.at[s, a, h])
                for a, (src, dst) in enumerate(((ck_hbm, k_buf), (cv_hbm, v_buf)))
                for h in range(n_heads)]

    @pl.when(b == 0)
    def _():
        for cp in head_copies(0, 0):
            cp.start()

    @pl.when(b + 1 < n_streams)
    def _():
        for cp in head_copies(b + 1, 1 - slot):
            cp.start()

    for cp in head_copies(b, slot):
        cp.wait()

    def past_rows(buf, head):
        return buf[slot, head].astype(BF16)

    ck_ref, cv_ref = k_buf, v_buf
    for g in range(n_heads // group):
        heads = range(g * group, (g + 1) * group)
        hsl = [slice(h * dh, (h + 1) * dh) for h in heads]
        z_past, z_new = [], []
        for h, hs in zip(heads, hsl):
            q = q_ref[:, hs]
            z_past.append(lax.dot_general(q, past_rows(ck_ref, h), NT_DIMS,
                                          preferred_element_type=F32))
            z_new.append(lax.dot_general(q, kn_ref[:, hs], NT_DIMS,
                                         preferred_element_type=F32) + bias_new)
        z_past = jnp.concatenate(z_past, axis=0)
        z_new = jnp.concatenate(z_new, axis=0)

        ls_new, lk_new = _sb_logs(z_new)
        w_new = jnp.exp2(ls_new + _sb_suffix(lk_new, u_new)).astype(BF16)
        off = jnp.sum(lk_new, axis=1, keepdims=True)
        ls_past, lk_past = _sb_logs(z_past)
        w_chunks = [None] * n_chunks
        for c in range(n_chunks - 1, -1, -1):
            cs = slice(c * chunk, (c + 1) * chunk)
            lk = lk_past[:, cs]
            w_chunks[c] = jnp.exp2(ls_past[:, cs] + _sb_suffix(lk, u_mat) + off).astype(BF16)
            off = off + jnp.sum(lk, axis=1, keepdims=True)
        w_past = jnp.concatenate(w_chunks, axis=1)

        for n, (h, hs) in enumerate(zip(heads, hsl)):
            rows = slice(n * t_len, (n + 1) * t_len)
            acc = (jnp.dot(w_past[rows], past_rows(cv_ref, h), preferred_element_type=F32)
                   + jnp.dot(w_new[rows], vn_ref[:, hs], preferred_element_type=F32))
            o_ref[:, hs] = (acc * za_ref[:, hs].astype(F32)).astype(o_ref.dtype)


def _attn_sample(q, ck, cv, kn, vn, za, *, row_offset, t_len, group, chunk):
    bsz, p_len, n_heads, dh = ck.shape
    width = q.shape[1]
    r = row_offset // t_len
    shifted = pl.BlockSpec((t_len, width), lambda b: (r + b, 0))
    local = pl.BlockSpec((t_len, width), lambda b: (b, 0))
    in_hbm = pl.BlockSpec(memory_space=pltpu.HBM)
    return pl.pallas_call(
        functools.partial(_attn_sample_body, group=group, chunk=chunk),
        grid=(bsz,),
        in_specs=[shifted, in_hbm, in_hbm, local, local, shifted],
        out_specs=local,
        out_shape=jax.ShapeDtypeStruct((bsz * t_len, width), BF16),
        scratch_shapes=[pltpu.VMEM((2, n_heads, p_len, dh), F32),
                        pltpu.VMEM((2, n_heads, p_len, dh), F32),
                        pltpu.SemaphoreType.DMA((2, 2, n_heads))],
        compiler_params=_params(1),
        name="attn_sample",
    )(q, ck, cv, kn, vn, za)


def _conv_body(u_ref, g_ref, st_ref, cw_ref, cb_ref, y_ref, ns_ref):
    u = u_ref[...]
    st = st_ref[0]
    t_len = u.shape[0]
    row = lax.broadcasted_iota(jnp.int32, u.shape, 0)
    u1 = jnp.where(row == 0, st[1:2, :], pltpu.roll(u, 1, axis=0))
    u2 = jnp.where(row == 0, st[0:1, :],
                   jnp.where(row == 1, st[1:2, :], pltpu.roll(u, 2, axis=0)))
    conv = cb_ref[...] + u2 * cw_ref[0:1, :] + u1 * cw_ref[1:2, :] + u * cw_ref[2:3, :]
    y_ref[...] = (g_ref[...].astype(F32) * conv).astype(y_ref.dtype)
    ns_ref[0] = u_ref[t_len - (CONV_WIDTH - 1):, :]


def _conv(u, gate, state, conv_w, conv_b, *, row_offset, t_len, tc):
    bsz = state.shape[0]
    ch = u.shape[1]
    tc = min(tc, ch)
    r = row_offset // t_len
    in_spec = pl.BlockSpec((t_len, tc), lambda b, c: (r + b, c))
    st_spec = pl.BlockSpec((1, CONV_WIDTH - 1, tc), lambda b, c: (b, 0, c))
    return pl.pallas_call(
        _conv_body,
        grid=(bsz, ch // tc),
        in_specs=[in_spec, in_spec, st_spec,
                  pl.BlockSpec((CONV_WIDTH, tc), lambda b, c: (0, c)),
                  pl.BlockSpec((1, tc), lambda b, c: (0, c))],
        out_specs=[pl.BlockSpec((t_len, tc), lambda b, c: (b, c)), st_spec],
        out_shape=[jax.ShapeDtypeStruct((bsz * t_len, ch), BF16),
                   jax.ShapeDtypeStruct((bsz, CONV_WIDTH - 1, ch), F32)],
        compiler_params=_params(2),
        name="short_conv",
    )(u, gate, state, conv_w, conv_b)


def _merge_body(ya_ref, yb_ref, wa_ref, wb_ref, ga_ref, gb_ref, o_ref):
    p_a = jnp.dot(ya_ref[...], wa_ref[...], preferred_element_type=F32)
    p_b = jnp.dot(yb_ref[...], wb_ref[...], preferred_element_type=F32)
    merged = ga_ref[...].astype(F32) * p_a + gb_ref[...].astype(F32) * p_b
    o_ref[...] = merged.astype(o_ref.dtype)


def _merge(y_a, y_b, w_a, w_b, gates, *, row_offset, tm, tn):
    m, ka = y_a.shape
    kb = y_b.shape[1]
    d = w_a.shape[1]
    tn = min(tn, d)
    nj = d // tn
    r = row_offset // tm
    return pl.pallas_call(
        _merge_body,
        grid=(m // tm, nj),
        in_specs=[pl.BlockSpec((tm, ka), lambda i, j: (i, 0)),
                  pl.BlockSpec((tm, kb), lambda i, j: (i, 0)),
                  pl.BlockSpec((ka, tn), lambda i, j: (0, j)),
                  pl.BlockSpec((kb, tn), lambda i, j: (0, j)),
                  pl.BlockSpec((tm, tn), lambda i, j: (r + i, j)),
                  pl.BlockSpec((tm, tn), lambda i, j: (r + i, nj + j))],
        out_specs=pl.BlockSpec((tm, tn), lambda i, j: (i, j)),
        out_shape=jax.ShapeDtypeStruct((m, d), BF16),
        compiler_params=_params(2),
        name="gated_merge",
    )(y_a, y_b, w_a, w_b, gates, gates)


def _out_body(m_ref, w_ref, x_ref, g_ref, b_ref, o_ref, *, alpha):
    pre = alpha * x_ref[...] + jnp.dot(m_ref[...], w_ref[...], preferred_element_type=F32)
    mu = jnp.mean(pre, axis=1, keepdims=True)
    cen = pre - mu
    var = jnp.mean(cen * cen, axis=1, keepdims=True)
    o_ref[...] = cen * lax.rsqrt(var + LN_EPS) * g_ref[...] + b_ref[...]


def _out_proj_ln(merged, w_o, x, ln_g, ln_b, *, alpha, tm):
    m, d = x.shape
    tm = min(tm, m)
    row_spec = lambda: pl.BlockSpec((tm, d), lambda i: (i, 0))
    const_spec = lambda r: pl.BlockSpec((r, d), lambda i: (0, 0), pipeline_mode=pl.Buffered(1))
    return pl.pallas_call(
        functools.partial(_out_body, alpha=alpha),
        grid=(m // tm,),
        in_specs=[row_spec(), const_spec(d), row_spec(), const_spec(1), const_spec(1)],
        out_specs=row_spec(),
        out_shape=jax.ShapeDtypeStruct((m, d), F32),
        compiler_params=_params(1, vmem_limit_bytes=OUT_PROJ_VMEM_LIMIT_BYTES),
        name="out_proj_ln",
    )(merged, w_o, x, ln_g, ln_b)


def _layer(x, past_k, past_v, past_conv, w_in, w_gates, b_in, conv_w, conv_b, w_a, w_b, w_o,
           ln_g, ln_b, rounded, *, n_heads, dh, alpha):
    bsz, t_len, d = x.shape
    m = bsz * t_len
    w_att = n_heads * dh
    w_conv = conv_w.shape[1]
    tm = min(ROW_TILE, m)
    xb = x.reshape(m, d).astype(BF16)
    q_scale = dh ** -0.5 * math.log2(math.e)

    o_q, o_k, o_v, o_za = 0, w_att, 2 * w_att, 3 * w_att
    o_bg = 4 * w_att
    o_cg, o_h, o_zb = o_bg + w_conv, o_bg + 2 * w_conv, o_bg + 3 * w_conv
    o_g = o_bg + 4 * w_conv
    ride = rounded is None
    gates_out = _proj(xb, w_gates, b_in[:, o_g:], [0], 2 * d, lambda a: (_sigmoid(a),), [BF16],
                      tm=tm, tn=1024, name="proj_gates",
                      round_cols=(w_in, o_g) if ride else None)
    gates = gates_out[-1]
    w_main = gates_out[0] if ride else rounded[0]

    proj = functools.partial(_proj, xb, w_main, b_in, tm=tm)
    (q,) = proj([o_q], w_att, lambda a: (a * q_scale,), [BF16], tn=1024, name="proj_q")
    k, kb16 = _proj_heads(xb, w_main, b_in, o_k, w_att, dh, tm=tm, tn=1024, name="proj_k")
    v, vb16 = _proj_heads(xb, w_main, b_in, o_v, w_att, dh, tm=tm, tn=1024, name="proj_v")
    za_out = proj([o_za], w_att, lambda a: (_silu(a),), [BF16], tn=1024, name="proj_za",
                  round_cols=(w_o, w_o.shape[1]) if ride else None)
    gate_b_out = proj([o_bg, o_zb], w_conv, lambda bg, zb: (_silu(zb) * bg,), [BF16], tn=512,
                      name="proj_gate_b", round_cols=(w_b, w_b.shape[1]) if ride else None)
    u_out = proj([o_cg, o_h], w_conv, lambda cg, h: (cg * h,), [F32], tn=512, name="proj_u",
                 round_cols=(w_a, w_a.shape[1]) if ride else None)
    za, gate_b, u = za_out[-1], gate_b_out[-1], u_out[-1]
    if ride:
        rounded = (w_main, u_out[0], gate_b_out[0], za_out[0])
    _, w_a, w_b, w_o = rounded

    if past_k is None:
        y_a = _attn_prompt(q, kb16, vb16, za, bsz=bsz, t_len=t_len, dh=dh, heads_per_step=2,
                           blk=ATT_BLOCK)
        past_conv = jnp.zeros((bsz, CONV_WIDTH - 1, w_conv), F32)
        tc = 512
    else:
        y_a = _attn_sample(q, past_k, past_v, kb16, vb16, za, row_offset=0, t_len=t_len,
                           group=4, chunk=ATT_BLOCK)
        tc = w_conv
    y_b, new_conv = _conv(u, gate_b, past_conv, conv_w, conv_b, row_offset=0, t_len=t_len, tc=tc)
    merged = _merge(y_a, y_b, w_a, w_b, gates, row_offset=0, tm=tm, tn=1024)
    y = _out_proj_ln(merged, w_o, x.reshape(m, d), ln_g, ln_b, alpha=alpha, tm=256)
    return (y.reshape(bsz, t_len, d), k.reshape(bsz, t_len, n_heads, dh),
            v.reshape(bsz, t_len, n_heads, dh), new_conv), rounded


def kernel(x_prompt, x_sample, cache_k, cache_v, state_conv, w_in, b_in, conv_w, conv_b,
           w_a, w_b, w_o, ln_g, ln_b):
    depth = w_in.shape[0]
    n_heads, dh = cache_k.shape[-2:]
    alpha = (2.0 * depth) ** 0.25
    xp, xs = x_prompt, x_sample
    kp, vp, cp, kn, vn, cn = [], [], [], [], [], []
    for l in range(depth):
        n_gate_cols = 2 * w_o.shape[-1]
        wts = (w_in[l], w_in[l][:, -n_gate_cols:].astype(BF16), b_in[l][None, :],
               conv_w[l], conv_b[l][None, :], w_a[l], w_b[l], w_o[l],
               ln_g[l][None, :], ln_b[l][None, :])
        layer = functools.partial(_layer, n_heads=n_heads, dh=dh, alpha=alpha)
        (xp, k1, v1, c1), rounded = layer(xp, None, None, None, *wts, None)
        (xs, k2, v2, c2), _ = layer(xs, cache_k[l], cache_v[l], state_conv[l], *wts, rounded)
        kp.append(k1); vp.append(v1); cp.append(c1)
        kn.append(k2); vn.append(v2); cn.append(c2)
    return (xp, xs, jnp.stack(kp), jnp.stack(vp), jnp.stack(cp),
            jnp.stack(kn), jnp.stack(vn), jnp.stack(cn))
```

```python
import functools
import math

import jax
import jax.numpy as jnp
from jax import lax
from jax.experimental import pallas as pl
from jax.experimental.pallas import tpu as pltpu

F32 = jnp.float32
BF16 = jnp.bfloat16

LN_EPS = 1e-5
CONV_WIDTH = 3
VMEM_LIMIT_BYTES = 58 * 1024 * 1024
OUT_PROJ_VMEM_LIMIT_BYTES = 60 * 1024 * 1024
LANES = 128
ATT_BLOCK = 256
ROW_TILE = 1024
MASKED_SCORE = -1e30
UNDERFLOW_LOG2 = -160.0
NT_DIMS = (((1,), (1,)), ((), ()))


def _params(n_grid_dims, vmem_limit_bytes=VMEM_LIMIT_BYTES):
    return pltpu.CompilerParams(
        dimension_semantics=("arbitrary",) * n_grid_dims,
        vmem_limit_bytes=vmem_limit_bytes,
    )


def _sigmoid(x):
    return 1.0 / (1.0 + jnp.exp(-x))


def _silu(x):
    return x * _sigmoid(x)


def _proj_body(*refs, n_slabs, n_out, epilogue):
    x_ref = refs[0]
    w_refs = refs[1:1 + n_slabs]
    b_refs = refs[1 + n_slabs:1 + 2 * n_slabs]
    o_refs = refs[-n_out:]
    x = x_ref[...]
    accs = [jnp.dot(x, w[...], preferred_element_type=F32) + b[...]
            for w, b in zip(w_refs, b_refs)]
    for o_ref, val in zip(o_refs, epilogue(*accs)):
        o_ref[...] = val.astype(o_ref.dtype)
    if len(refs) > 1 + 2 * n_slabs + n_out:
        refs[1 + 2 * n_slabs + 1][...] = refs[1 + 2 * n_slabs][...].astype(BF16)


def _proj(x, w, b, col_offsets, width, epilogue, out_dtypes, *, tm, tn, name,
          round_cols=None):
    m, k = x.shape
    n_slabs = len(col_offsets)
    tn = min(tn, width)
    ni, nj = m // tm, width // tn
    in_specs = [pl.BlockSpec((tm, k), lambda i, j: (i, 0))]
    for off in col_offsets:
        in_specs.append(pl.BlockSpec((k, tn), lambda i, j, o=off // tn: (0, o + j)))
    for off in col_offsets:
        in_specs.append(pl.BlockSpec((1, tn), lambda i, j, o=off // tn: (0, o + j)))
    out_specs = [pl.BlockSpec((tm, tn), lambda i, j: (i, j)) for _ in out_dtypes]
    out_shape = [jax.ShapeDtypeStruct((m, width), dt) for dt in out_dtypes]
    operands = [x, *([w] * n_slabs), *([b] * n_slabs)]
    if round_cols is not None:
        src, n_cols = round_cols
        chunk, rem = divmod(n_cols, ni * nj)
        assert rem == 0 and chunk % LANES == 0, (n_cols, ni, nj)
        chunk_spec = pl.BlockSpec((src.shape[0], chunk), lambda i, j: (0, i * nj + j))
        in_specs.append(chunk_spec)
        out_specs.insert(0, chunk_spec)
        out_shape.insert(0, jax.ShapeDtypeStruct((src.shape[0], n_cols), BF16))
        operands.append(src)
    return pl.pallas_call(
        functools.partial(_proj_body, n_slabs=n_slabs, n_out=len(out_dtypes), epilogue=epilogue),
        grid=(ni, nj),
        in_specs=in_specs,
        out_specs=out_specs,
        out_shape=out_shape,
        compiler_params=_params(2),
        name=name,
    )(*operands)


def _proj_heads_body(x_ref, w_ref, b_ref, of_ref, ob_ref):
    acc = jnp.dot(x_ref[...], w_ref[...], preferred_element_type=F32) + b_ref[...]
    ob_ref[...] = acc.astype(ob_ref.dtype)
    _, heads, dh = of_ref.shape
    for h in range(heads):
        of_ref[:, h, :] = acc[:, h * dh:(h + 1) * dh]


def _proj_heads(x, w, b, col_offset, width, dh, *, tm, tn, name):
    m, k = x.shape
    tn = min(tn, width)
    o = col_offset // tn
    return pl.pallas_call(
        _proj_heads_body,
        grid=(m // tm, width // tn),
        in_specs=[pl.BlockSpec((tm, k), lambda i, j: (i, 0)),
                  pl.BlockSpec((k, tn), lambda i, j: (0, o + j)),
                  pl.BlockSpec((1, tn), lambda i, j: (0, o + j))],
        out_specs=[pl.BlockSpec((tm, tn // dh, dh), lambda i, j: (i, j, 0)),
                   pl.BlockSpec((tm, tn), lambda i, j: (i, j))],
        out_shape=[jax.ShapeDtypeStruct((m, width // dh, dh), F32),
                   jax.ShapeDtypeStruct((m, width), BF16)],
        compiler_params=_params(2),
        name=name,
    )(x, w, b)


def _suffix_ones(n):
    j = lax.broadcasted_iota(jnp.int32, (n, n), 0)
    s = lax.broadcasted_iota(jnp.int32, (n, n), 1)
    return (j > s).astype(BF16)


def _causal_bias(n):
    row = lax.broadcasted_iota(jnp.int32, (n, n), 0)
    col = lax.broadcasted_iota(jnp.int32, (n, n), 1)
    return jnp.where(col < row, 0.0, MASKED_SCORE).astype(F32)


def _sb_logs(z):
    ls = jnp.minimum(z, 0.0) - jnp.log2(1.0 + jnp.exp2(-jnp.abs(z)))
    return ls, ls - z


def _add_lane_tiled(a, r):
    return jnp.concatenate(
        [a[:, c:c + LANES] + r for c in range(0, a.shape[1], LANES)], axis=1)


def _sb_suffix(lk, u_mat):
    return jnp.dot(lk.astype(BF16), u_mat, preferred_element_type=F32)


def _attn_prompt_body(q_ref, k_ref, v_ref, za_ref, o_ref,
                      bias_ref, z_buf, tw_buf, af_buf, run_ref, acc_ref, *, heads, dh, blk):
    t_len = q_ref.shape[0]
    nq = t_len // blk
    u_mat = _suffix_ones(blk)
    bias_ref[0:blk, :] = _causal_bias(blk)
    bias_ref[blk:2 * blk, :] = jnp.zeros((blk, blk), F32)
    hsl = [slice(h * dh, (h + 1) * dh) for h in range(heads)]
    for h in range(heads):
        z_buf[1, h] = jnp.full((blk, blk), MASKED_SCORE, F32)
        tw_buf[1, h] = jnp.full((blk, blk), MASKED_SCORE, F32)
        af_buf[1, h] = jnp.zeros((blk, blk), F32)
        run_ref[h] = jnp.zeros((blk, LANES), F32)
        acc_ref[h] = jnp.zeros((blk, dh), F32)

    def step(st, new):
        (qa, ka), (qb, kb, last_b, valid_b), (qc, kc, last_c, valid_c) = st
        old = 1 - new
        valid_a = qa < nq
        q0 = pl.multiple_of(jnp.minimum(qa, nq - 1) * blk, blk)
        k0 = pl.multiple_of(jnp.minimum(ka, nq - 1) * blk, blk)
        b0 = pl.multiple_of(jnp.where(ka == qa, 0, blk), blk)
        bias = bias_ref[pl.ds(b0, blk), :]
        z_new = [lax.dot_general(q_ref[pl.ds(q0, blk), hsl[h]],
                                 k_ref[pl.ds(k0, blk), hsl[h]], NT_DIMS,
                                 preferred_element_type=F32) for h in range(heads)]
        first = kb == qb
        af_new = []
        run_max = None
        for h in range(heads):
            ls, lk = _sb_logs(z_buf[old, h])
            off = jnp.where(first, 0.0, run_ref[h])
            off = jnp.where(valid_b == 1, off, MASKED_SCORE)
            tw_buf[new, h] = _add_lane_tiled(ls, off)
            af_new.append(_sb_suffix(lk, u_mat))
            run = off + jnp.broadcast_to(jnp.sum(lk, axis=1, keepdims=True), off.shape)
            run_ref[h] = run
            run_max = jnp.max(run) if run_max is None else jnp.maximum(run_max, jnp.max(run))
        vc0 = pl.multiple_of(kc * blk, blk)
        qc0 = pl.multiple_of(qc * blk, blk)
        restart = jnp.logical_and(last_c == 1, qc < nq - 1)
        for h in range(heads):
            w = jnp.exp2(tw_buf[old, h] + af_buf[old, h]).astype(BF16)
            acc = acc_ref[h] + jnp.dot(w, v_ref[pl.ds(vc0, blk), hsl[h]],
                                       preferred_element_type=F32)
            gate = za_ref[pl.ds(qc0, blk), hsl[h]].astype(F32)
            o_ref[pl.ds(qc0, blk), hsl[h]] = (acc * gate).astype(o_ref.dtype)
            acc_ref[h] = jnp.where(restart, 0.0, acc)
        for h in range(heads):
            z_buf[new, h] = z_new[h] + bias
            af_buf[new, h] = af_new[h]
        dead = jnp.logical_and(valid_b == 1, run_max < UNDERFLOW_LOG2)
        ends = jnp.logical_or(ka == 0, jnp.logical_and(dead, qa == qb))
        nxt_q = jnp.where(jnp.logical_and(ends, valid_a), qa + 1, qa)
        nxt_k = jnp.where(ends, nxt_q, ka - 1)
        return ((nxt_q, nxt_k),
                (jnp.minimum(qa, nq - 1), jnp.minimum(ka, nq - 1), ends.astype(jnp.int32),
                 valid_a.astype(jnp.int32)),
                (qb, kb, last_b, valid_b))

    def in_flight(st):
        (qa, _), (_, _, _, valid_b), (_, _, _, valid_c) = st
        return jnp.logical_or(qa < nq, jnp.logical_or(valid_b == 1, valid_c == 1))

    zero = jnp.int32(0)
    lax.while_loop(in_flight, lambda st: step(step(st, 0), 1),
                   ((zero, zero), (zero, zero, zero, zero), (zero, zero, zero, zero)))


def _attn_prompt(q, k, v, za, *, bsz, t_len, dh, heads_per_step, blk):
    width = q.shape[1]
    hw = heads_per_step * dh
    spec = pl.BlockSpec((t_len, hw), lambda b, g: (b, g))
    return pl.pallas_call(
        functools.partial(_attn_prompt_body, heads=heads_per_step, dh=dh, blk=blk),
        grid=(bsz, width // hw),
        in_specs=[spec, spec, spec, spec],
        out_specs=spec,
        out_shape=jax.ShapeDtypeStruct((bsz * t_len, width), BF16),
        scratch_shapes=[pltpu.VMEM((2 * blk, blk), F32),
                        pltpu.VMEM((2, heads_per_step, blk, blk), F32),
                        pltpu.VMEM((2, heads_per_step, blk, blk), F32),
                        pltpu.VMEM((2, heads_per_step, blk, blk), F32),
                        pltpu.VMEM((heads_per_step, blk, LANES), F32),
                        pltpu.VMEM((heads_per_step, blk, dh), F32)],
        compiler_params=_params(2),
        name="attn_prompt",
    )(q, k, v, za)


def _attn_sample_body(q_ref, ck_hbm, cv_hbm, kn_ref, vn_ref, za_ref, o_ref, k_buf, v_buf, sems,
                      *, group, chunk):
    b = pl.program_id(0)
    n_streams = pl.num_programs(0)
    t_len = q_ref.shape[0]
    _, n_heads, p_len, dh = k_buf.shape
    n_chunks = p_len // chunk
    u_mat = _suffix_ones(chunk)
    u_new = _suffix_ones(t_len)
    bias_new = _causal_bias(t_len)
    slot = lax.rem(b, 2)

    def head_copies(stream, s):
        return [pltpu.make_async_copy(src.at[stream, :, h, :], dst.at[s, h], sems.at[s, a, h])
                for a, (src, dst) in enumerate(((ck_hbm, k_buf), (cv_hbm, v_buf)))
                for h in range(n_heads)]

    @pl.when(b == 0)
    def _():
        for cp in head_copies(0, 0):
            cp.start()

    @pl.when(b + 1 < n_streams)
    def _():
        for cp in head_copies(b + 1, 1 - slot):
            cp.start()

    for cp in head_copies(b, slot):
        cp.wait()

    def past_rows(buf, head):
        return buf[slot, head].astype(BF16)

    ck_ref, cv_ref = k_buf, v_buf
    for g in range(n_heads // group):
        heads = range(g * group, (g + 1) * group)
        hsl = [slice(h * dh, (h + 1) * dh) for h in heads]
        z_past, z_new = [], []
        for h, hs in zip(heads, hsl):
            q = q_ref[:, hs]
            z_past.append(lax.dot_general(q, past_rows(ck_ref, h), NT_DIMS,
                                          preferred_element_type=F32))
            z_new.append(lax.dot_general(q, kn_ref[:, hs], NT_DIMS,
                                         preferred_element_type=F32) + bias_new)
        z_past = jnp.concatenate(z_past, axis=0)
        z_new = jnp.concatenate(z_new, axis=0)

        ls_new, lk_new = _sb_logs(z_new)
        w_new = jnp.exp2(ls_new + _sb_suffix(lk_new, u_new)).astype(BF16)
        off = jnp.sum(lk_new, axis=1, keepdims=True)
        ls_past, lk_past = _sb_logs(z_past)
        w_chunks = [None] * n_chunks
        for c in range(n_chunks - 1, -1, -1):
            cs = slice(c * chunk, (c + 1) * chunk)
            lk = lk_past[:, cs]
            w_chunks[c] = jnp.exp2(ls_past[:, cs] + _sb_suffix(lk, u_mat) + off).astype(BF16)
            off = off + jnp.sum(lk, axis=1, keepdims=True)
        w_past = jnp.concatenate(w_chunks, axis=1)

        for n, (h, hs) in enumerate(zip(heads, hsl)):
            rows = slice(n * t_len, (n + 1) * t_len)
            acc = (jnp.dot(w_past[rows], past_rows(cv_ref, h), preferred_element_type=F32)
                   + jnp.dot(w_new[rows], vn_ref[:, hs], preferred_element_type=F32))
            o_ref[:, hs] = (acc * za_ref[:, hs].astype(F32)).astype(o_ref.dtype)


def _attn_sample(q, ck, cv, kn, vn, za, *, row_offset, t_len, group, chunk):
    bsz, p_len, n_heads, dh = ck.shape
    width = q.shape[1]
    r = row_offset // t_len
    shifted = pl.BlockSpec((t_len, width), lambda b: (r + b, 0))
    local = pl.BlockSpec((t_len, width), lambda b: (b, 0))
    in_hbm = pl.BlockSpec(memory_space=pltpu.HBM)
    return pl.pallas_call(
        functools.partial(_attn_sample_body, group=group, chunk=chunk),
        grid=(bsz,),
        in_specs=[shifted, in_hbm, in_hbm, local, local, shifted],
        out_specs=local,
        out_shape=jax.ShapeDtypeStruct((bsz * t_len, width), BF16),
        scratch_shapes=[pltpu.VMEM((2, n_heads, p_len, dh), F32),
                        pltpu.VMEM((2, n_heads, p_len, dh), F32),
                        pltpu.SemaphoreType.DMA((2, 2, n_heads))],
        compiler_params=_params(1),
        name="attn_sample",
    )(q, ck, cv, kn, vn, za)


def _conv_body(u_ref, g_ref, st_ref, cw_ref, cb_ref, y_ref, ns_ref):
    u = u_ref[...]
    st = st_ref[0]
    t_len = u.shape[0]
    row = lax.broadcasted_iota(jnp.int32, u.shape, 0)
    u1 = jnp.where(row == 0, st[1:2, :], pltpu.roll(u, 1, axis=0))
    u2 = jnp.where(row == 0, st[0:1, :],
                   jnp.where(row == 1, st[1:2, :], pltpu.roll(u, 2, axis=0)))
    conv = cb_ref[...] + u2 * cw_ref[0:1, :] + u1 * cw_ref[1:2, :] + u * cw_ref[2:3, :]
    y_ref[...] = (g_ref[...].astype(F32) * conv).astype(y_ref.dtype)
    ns_ref[0] = u_ref[t_len - (CONV_WIDTH - 1):, :]


def _conv(u, gate, state, conv_w, conv_b, *, row_offset, t_len, tc):
    bsz = state.shape[0]
    ch = u.shape[1]
    tc = min(tc, ch)
    r = row_offset // t_len
    in_spec = pl.BlockSpec((t_len, tc), lambda b, c: (r + b, c))
    st_spec = pl.BlockSpec((1, CONV_WIDTH - 1, tc), lambda b, c: (b, 0, c))
    return pl.pallas_call(
        _conv_body,
        grid=(bsz, ch // tc),
        in_specs=[in_spec, in_spec, st_spec,
                  pl.BlockSpec((CONV_WIDTH, tc), lambda b, c: (0, c)),
                  pl.BlockSpec((1, tc), lambda b, c: (0, c))],
        out_specs=[pl.BlockSpec((t_len, tc), lambda b, c: (b, c)), st_spec],
        out_shape=[jax.ShapeDtypeStruct((bsz * t_len, ch), BF16),
                   jax.ShapeDtypeStruct((bsz, CONV_WIDTH - 1, ch), F32)],
        compiler_params=_params(2),
        name="short_conv",
    )(u, gate, state, conv_w, conv_b)


def _merge_body(ya_ref, yb_ref, wa_ref, wb_ref, ga_ref, gb_ref, o_ref):
    p_a = jnp.dot(ya_ref[...], wa_ref[...], preferred_element_type=F32)
    p_b = jnp.dot(yb_ref[...], wb_ref[...], preferred_element_type=F32)
    merged = ga_ref[...].astype(F32) * p_a + gb_ref[...].astype(F32) * p_b
    o_ref[...] = merged.astype(o_ref.dtype)


def _merge(y_a, y_b, w_a, w_b, gates, *, row_offset, tm, tn):
    m, ka = y_a.shape
    kb = y_b.shape[1]
    d = w_a.shape[1]
    tn = min(tn, d)
    nj = d // tn
    r = row_offset // tm
    return pl.pallas_call(
        _merge_body,
        grid=(m // tm, nj),
        in_specs=[pl.BlockSpec((tm, ka), lambda i, j: (i, 0)),
                  pl.BlockSpec((tm, kb), lambda i, j: (i, 0)),
                  pl.BlockSpec((ka, tn), lambda i, j: (0, j)),
                  pl.BlockSpec((kb, tn), lambda i, j: (0, j)),
                  pl.BlockSpec((tm, tn), lambda i, j: (r + i, j)),
                  pl.BlockSpec((tm, tn), lambda i, j: (r + i, nj + j))],
        out_specs=pl.BlockSpec((tm, tn), lambda i, j: (i, j)),
        out_shape=jax.ShapeDtypeStruct((m, d), BF16),
        compiler_params=_params(2),
        name="gated_merge",
    )(y_a, y_b, w_a, w_b, gates, gates)


def _out_body(m_ref, w_ref, x_ref, g_ref, b_ref, o_ref, *, alpha):
    pre = alpha * x_ref[...] + jnp.dot(m_ref[...], w_ref[...], preferred_element_type=F32)
    mu = jnp.mean(pre, axis=1, keepdims=True)
    cen = pre - mu
    var = jnp.mean(cen * cen, axis=1, keepdims=True)
    o_ref[...] = cen * lax.rsqrt(var + LN_EPS) * g_ref[...] + b_ref[...]


def _out_proj_ln(merged, w_o, x, ln_g, ln_b, *, alpha, tm):
    m, d = x.shape
    tm = min(tm, m)
    row_spec = lambda: pl.BlockSpec((tm, d), lambda i: (i, 0))
    const_spec = lambda r: pl.BlockSpec((r, d), lambda i: (0, 0), pipeline_mode=pl.Buffered(1))
    return pl.pallas_call(
        functools.partial(_out_body, alpha=alpha),
        grid=(m // tm,),
        in_specs=[row_spec(), const_spec(d), row_spec(), const_spec(1), const_spec(1)],
        out_specs=row_spec(),
        out_shape=jax.ShapeDtypeStruct((m, d), F32),
        compiler_params=_params(1, vmem_limit_bytes=OUT_PROJ_VMEM_LIMIT_BYTES),
        name="out_proj_ln",
    )(merged, w_o, x, ln_g, ln_b)


def _layer(x, past_k, past_v, past_conv, w_in, w_gates, b_in, conv_w, conv_b, w_a, w_b, w_o,
           ln_g, ln_b, rounded, *, n_heads, dh, alpha):
    bsz, t_len, d = x.shape
    m = bsz * t_len
    w_att = n_heads * dh
    w_conv = conv_w.shape[1]
    tm = min(ROW_TILE, m)
    xb = x.reshape(m, d).astype(BF16)
    q_scale = dh ** -0.5 * math.log2(math.e)

    o_q, o_k, o_v, o_za = 0, w_att, 2 * w_att, 3 * w_att
    o_bg = 4 * w_att
    o_cg, o_h, o_zb = o_bg + w_conv, o_bg + 2 * w_conv, o_bg + 3 * w_conv
    o_g = o_bg + 4 * w_conv
    ride = rounded is None
    gates_out = _proj(xb, w_gates, b_in[:, o_g:], [0], 2 * d, lambda a: (_sigmoid(a),), [BF16],
                      tm=tm, tn=1024, name="proj_gates",
                      round_cols=(w_in, o_g) if ride else None)
    gates = gates_out[-1]
    w_main = gates_out[0] if ride else rounded[0]

    proj = functools.partial(_proj, xb, w_main, b_in, tm=tm)
    (q,) = proj([o_q], w_att, lambda a: (a * q_scale,), [BF16], tn=1024, name="proj_q")
    k, kb16 = _proj_heads(xb, w_main, b_in, o_k, w_att, dh, tm=tm, tn=1024, name="proj_k")
    v, vb16 = _proj_heads(xb, w_main, b_in, o_v, w_att, dh, tm=tm, tn=1024, name="proj_v")
    za_out = proj([o_za], w_att, lambda a: (_silu(a),), [BF16], tn=1024, name="proj_za",
                  round_cols=(w_o, w_o.shape[1]) if ride else None)
    gate_b_out = proj([o_bg, o_zb], w_conv, lambda bg, zb: (_silu(zb) * bg,), [BF16], tn=512,
                      name="proj_gate_b", round_cols=(w_b, w_b.shape[1]) if ride else None)
    u_out = proj([o_cg, o_h], w_conv, lambda cg, h: (cg * h,), [F32], tn=512, name="proj_u",
                 round_cols=(w_a, w_a.shape[1]) if ride else None)
    za, gate_b, u = za_out[-1], gate_b_out[-1], u_out[-1]
    if ride:
        rounded = (w_main, u_out[0], gate_b_out[0], za_out[0])
    _, w_a, w_b, w_o = rounded

    if past_k is None:
        y_a = _attn_prompt(q, kb16, vb16, za, bsz=bsz, t_len=t_len, dh=dh, heads_per_step=2,
                           blk=ATT_BLOCK)
        past_conv = jnp.zeros((bsz, CONV_WIDTH - 1, w_conv), F32)
        tc = 512
    else:
        y_a = _attn_sample(q, past_k, past_v, kb16, vb16, za, row_offset=0, t_len=t_len,
                           group=4, chunk=ATT_BLOCK)
        tc = w_conv
    y_b, new_conv = _conv(u, gate_b, past_conv, conv_w, conv_b, row_offset=0, t_len=t_len, tc=tc)
    merged = _merge(y_a, y_b, w_a, w_b, gates, row_offset=0, tm=tm, tn=1024)
    y = _out_proj_ln(merged, w_o, x.reshape(m, d), ln_g, ln_b, alpha=alpha, tm=256)
    return (y.reshape(bsz, t_len, d), k.reshape(bsz, t_len, n_heads, dh),
            v.reshape(bsz, t_len, n_heads, dh), new_conv), rounded


def kernel(x_prompt, x_sample, cache_k, cache_v, state_conv, w_in, b_in, conv_w, conv_b,
           w_a, w_b, w_o, ln_g, ln_b):
    depth = w_in.shape[0]
    n_heads, dh = cache_k.shape[-2:]
    alpha = (2.0 * depth) ** 0.25
    xp, xs = x_prompt, x_sample
    kp, vp, cp, kn, vn, cn = [], [], [], [], [], []
    for l in range(depth):
        n_gate_cols = 2 * w_o.shape[-1]
        wts = (w_in[l], w_in[l][:, -n_gate_cols:].astype(BF16), b_in[l][None, :],
               conv_w[l], conv_b[l][None, :], w_a[l], w_b[l], w_o[l],
               ln_g[l][None, :], ln_b[l][None, :])
        layer = functools.partial(_layer, n_heads=n_heads, dh=dh, alpha=alpha)
        (xp, k1, v1, c1), rounded = layer(xp, None, None, None, *wts, None)
        (xs, k2, v2, c2), _ = layer(xs, cache_k[l], cache_v[l], state_conv[l], *wts, rounded)
        kp.append(k1); vp.append(v1); cp.append(c1)
        kn.append(k2); vn.append(v2); cn.append(c2)
    return (xp, xs, jnp.stack(kp), jnp.stack(vp), jnp.stack(cp),
            jnp.stack(kn), jnp.stack(vn), jnp.stack(cn))
```

```python
import functools
import math

import jax
import jax.numpy as jnp
from jax import lax
from jax.experimental import pallas as pl
from jax.experimental.pallas import tpu as pltpu

F32 = jnp.float32
BF16 = jnp.bfloat16

LN_EPS = 1e-5
CONV_WIDTH = 3
VMEM_LIMIT_BYTES = 58 * 1024 * 1024
OUT_PROJ_VMEM_LIMIT_BYTES = 60 * 1024 * 1024
LANES = 128
ATT_BLOCK = 256
ROW_TILE = 1024
MASKED_SCORE = -1e30
UNDERFLOW_LOG2 = -160.0
NT_DIMS = (((1,), (1,)), ((), ()))


def _params(n_grid_dims, vmem_limit_bytes=VMEM_LIMIT_BYTES):
    return pltpu.CompilerParams(
        dimension_semantics=("arbitrary",) * n_grid_dims,
        vmem_limit_bytes=vmem_limit_bytes,
    )


def _sigmoid(x):
    return 1.0 / (1.0 + jnp.exp(-x))


def _silu(x):
    return x * _sigmoid(x)


def _proj_body(*refs, n_slabs, n_out, round_x, round_w, epilogue):
    x_ref = refs[0]
    w_refs = refs[1:1 + n_slabs]
    b_refs = refs[1 + n_slabs:1 + 2 * n_slabs]
    o_refs = refs[-n_out:]
    side = list(refs[1 + 2 * n_slabs:-n_out])
    if round_w:
        side[-1][...] = side[0][...].astype(BF16)
    if round_x:
        xb_ref = side[1 if round_w else 0]

        @pl.when(pl.program_id(1) == 0)
        def _():
            xb_ref[...] = x_ref[...].astype(BF16)

        x = xb_ref[...]
    else:
        x = x_ref[...]
    accs = [jnp.dot(x, w[...], preferred_element_type=F32) + b[...]
            for w, b in zip(w_refs, b_refs)]
    for o_ref, val in zip(o_refs, epilogue(*accs)):
        o_ref[...] = val.astype(o_ref.dtype)


def _proj(x, w, b, col_offsets, width, epilogue, out_dtypes, *, tm, tn, name,
          round_cols=None):
    m, k = x.shape
    n_slabs = len(col_offsets)
    tn = min(tn, width)
    ni, nj = m // tm, width // tn
    in_specs = [pl.BlockSpec((tm, k), lambda i, j: (i, 0))]
    for off in col_offsets:
        in_specs.append(pl.BlockSpec((k, tn), lambda i, j, o=off // tn: (0, o + j)))
    for off in col_offsets:
        in_specs.append(pl.BlockSpec((1, tn), lambda i, j, o=off // tn: (0, o + j)))
    out_specs = [pl.BlockSpec((tm, tn), lambda i, j: (i, j)) for _ in out_dtypes]
    out_shape = [jax.ShapeDtypeStruct((m, width), dt) for dt in out_dtypes]
    operands = [x, *([w] * n_slabs), *([b] * n_slabs)]
    round_x = x.dtype == F32
    if round_cols is not None:
        src, n_cols = round_cols
        chunk, rem = divmod(n_cols, ni * nj)
        assert rem == 0 and chunk % LANES == 0, (n_cols, ni, nj)
        chunk_spec = pl.BlockSpec((src.shape[0], chunk), lambda i, j: (0, i * nj + j))
        in_specs.append(chunk_spec)
        out_specs.insert(0, chunk_spec)
        out_shape.insert(0, jax.ShapeDtypeStruct((src.shape[0], n_cols), BF16))
        operands.append(src)
    if round_x:
        out_specs.insert(0, pl.BlockSpec((tm, k), lambda i, j: (i, 0)))
        out_shape.insert(0, jax.ShapeDtypeStruct((m, k), BF16))
    return pl.pallas_call(
        functools.partial(_proj_body, n_slabs=n_slabs, n_out=len(out_dtypes), round_x=round_x,
                          round_w=round_cols is not None, epilogue=epilogue),
        grid=(ni, nj),
        in_specs=in_specs,
        out_specs=out_specs,
        out_shape=out_shape,
        compiler_params=_params(2),
        name=name,
    )(*operands)


def _proj_heads_body(x_ref, w_ref, b_ref, of_ref, ob_ref):
    acc = jnp.dot(x_ref[...], w_ref[...], preferred_element_type=F32) + b_ref[...]
    ob_ref[...] = acc.astype(ob_ref.dtype)
    of_ref[...] = acc.reshape(of_ref.shape)


def _proj_heads(x, w, b, col_offset, width, dh, *, tm, tn, name):
    m, k = x.shape
    tn = min(tn, width)
    o = col_offset // tn
    return pl.pallas_call(
        _proj_heads_body,
        grid=(m // tm, width // tn),
        in_specs=[pl.BlockSpec((tm, k), lambda i, j: (i, 0)),
                  pl.BlockSpec((k, tn), lambda i, j: (0, o + j)),
                  pl.BlockSpec((1, tn), lambda i, j: (0, o + j))],
        out_specs=[pl.BlockSpec((tm, tn // dh, dh), lambda i, j: (i, j, 0)),
                   pl.BlockSpec((tm, tn), lambda i, j: (i, j))],
        out_shape=[jax.ShapeDtypeStruct((m, width // dh, dh), F32),
                   jax.ShapeDtypeStruct((m, width), BF16)],
        compiler_params=_params(2),
        name=name,
    )(x, w, b)


def _suffix_ones(n):
    j = lax.broadcasted_iota(jnp.int32, (n, n), 0)
    s = lax.broadcasted_iota(jnp.int32, (n, n), 1)
    return (j > s).astype(BF16)


def _causal_bias(n):
    row = lax.broadcasted_iota(jnp.int32, (n, n), 0)
    col = lax.broadcasted_iota(jnp.int32, (n, n), 1)
    return jnp.where(col < row, 0.0, MASKED_SCORE).astype(F32)


def _sb_logs(z):
    ls = jnp.minimum(z, 0.0) - jnp.log2(1.0 + jnp.exp2(-jnp.abs(z)))
    return ls, ls - z


def _add_lane_tiled(a, r):
    return jnp.concatenate(
        [a[:, c:c + LANES] + r for c in range(0, a.shape[1], LANES)], axis=1)


def _sb_suffix(lk, u_mat):
    return jnp.dot(lk.astype(BF16), u_mat, preferred_element_type=F32)


def _attn_prompt_body(q_ref, k_ref, v_ref, za_ref, o_ref,
                      bias_ref, z_buf, tw_buf, af_buf, run_ref, acc_ref, *, heads, dh, blk):
    t_len = q_ref.shape[0]
    nq = t_len // blk
    u_mat = _suffix_ones(blk)
    bias_ref[0:blk, :] = _causal_bias(blk)
    bias_ref[blk:2 * blk, :] = jnp.zeros((blk, blk), F32)
    hsl = [slice(h * dh, (h + 1) * dh) for h in range(heads)]
    for h in range(heads):
        z_buf[1, h] = jnp.full((blk, blk), MASKED_SCORE, F32)
        tw_buf[1, h] = jnp.full((blk, blk), MASKED_SCORE, F32)
        af_buf[1, h] = jnp.zeros((blk, blk), F32)
        run_ref[h] = jnp.zeros((blk, LANES), F32)
        acc_ref[h] = jnp.zeros((blk, dh), F32)

    def step(st, new):
        (qa, ka), (qb, kb, last_b, valid_b), (qc, kc, last_c, valid_c) = st
        old = 1 - new
        valid_a = qa < nq
        q0 = pl.multiple_of(jnp.minimum(qa, nq - 1) * blk, blk)
        k0 = pl.multiple_of(jnp.minimum(ka, nq - 1) * blk, blk)
        b0 = pl.multiple_of(jnp.where(ka == qa, 0, blk), blk)
        bias = bias_ref[pl.ds(b0, blk), :]
        z_new = [lax.dot_general(q_ref[pl.ds(q0, blk), hsl[h]],
                                 k_ref[pl.ds(k0, blk), hsl[h]], NT_DIMS,
                                 preferred_element_type=F32) for h in range(heads)]
        first = kb == qb
        af_new = []
        run_max = None
        for h in range(heads):
            ls, lk = _sb_logs(z_buf[old, h])
            off = jnp.where(first, 0.0, run_ref[h])
            off = jnp.where(valid_b == 1, off, MASKED_SCORE)
            tw_buf[new, h] = _add_lane_tiled(ls, off)
            af_new.append(_sb_suffix(lk, u_mat))
            run = off + jnp.broadcast_to(jnp.sum(lk, axis=1, keepdims=True), off.shape)
            run_ref[h] = run
            run_max = jnp.max(run) if run_max is None else jnp.maximum(run_max, jnp.max(run))
        vc0 = pl.multiple_of(kc * blk, blk)
        qc0 = pl.multiple_of(qc * blk, blk)
        restart = jnp.logical_and(last_c == 1, qc < nq - 1)
        for h in range(heads):
            w = jnp.exp2(tw_buf[old, h] + af_buf[old, h]).astype(BF16)
            acc = acc_ref[h] + jnp.dot(w, v_ref[pl.ds(vc0, blk), hsl[h]],
                                       preferred_element_type=F32)
            gate = za_ref[pl.ds(qc0, blk), hsl[h]].astype(F32)
            o_ref[pl.ds(qc0, blk), hsl[h]] = (acc * gate).astype(o_ref.dtype)
            acc_ref[h] = jnp.where(restart, 0.0, acc)
        for h in range(heads):
            z_buf[new, h] = z_new[h] + bias
            af_buf[new, h] = af_new[h]
        dead = jnp.logical_and(valid_b == 1, run_max < UNDERFLOW_LOG2)
        ends = jnp.logical_or(ka == 0, jnp.logical_and(dead, qa == qb))
        nxt_q = jnp.where(jnp.logical_and(ends, valid_a), qa + 1, qa)
        nxt_k = jnp.where(ends, nxt_q, ka - 1)
        return ((nxt_q, nxt_k),
                (jnp.minimum(qa, nq - 1), jnp.minimum(ka, nq - 1), ends.astype(jnp.int32),
                 valid_a.astype(jnp.int32)),
                (qb, kb, last_b, valid_b))

    def in_flight(st):
        (qa, _), (_, _, _, valid_b), (_, _, _, valid_c) = st
        return jnp.logical_or(qa < nq, jnp.logical_or(valid_b == 1, valid_c == 1))

    zero = jnp.int32(0)
    lax.while_loop(in_flight, lambda st: step(step(st, 0), 1),
                   ((zero, zero), (zero, zero, zero, zero), (zero, zero, zero, zero)))


def _attn_prompt(q, k, v, za, *, bsz, t_len, dh, heads_per_step, blk):
    width = q.shape[1]
    hw = heads_per_step * dh
    spec = pl.BlockSpec((t_len, hw), lambda b, g: (b, g))
    return pl.pallas_call(
        functools.partial(_attn_prompt_body, heads=heads_per_step, dh=dh, blk=blk),
        grid=(bsz, width // hw),
        in_specs=[spec, spec, spec, spec],
        out_specs=spec,
        out_shape=jax.ShapeDtypeStruct((bsz * t_len, width), BF16),
        scratch_shapes=[pltpu.VMEM((2 * blk, blk), F32),
                        pltpu.VMEM((2, heads_per_step, blk, blk), F32),
                        pltpu.VMEM((2, heads_per_step, blk, blk), F32),
                        pltpu.VMEM((2, heads_per_step, blk, blk), F32),
                        pltpu.VMEM((heads_per_step, blk, LANES), F32),
                        pltpu.VMEM((heads_per_step, blk, dh), F32)],
        compiler_params=_params(2),
        name="attn_prompt",
    )(q, k, v, za)


def _attn_sample_body(q_ref, ck_hbm, cv_hbm, kn_ref, vn_ref, za_ref, o_ref, k_buf, v_buf, sems,
                      *, group, chunk):
    b = pl.program_id(0)
    n_streams = pl.num_programs(0)
    t_len = q_ref.shape[0]
    _, n_heads, p_len, dh = k_buf.shape
    n_chunks = p_len // chunk
    u_mat = _suffix_ones(chunk)
    u_new = _suffix_ones(t_len)
    bias_new = _causal_bias(t_len)
    slot = lax.rem(b, 2)

    def head_copies(stream, s):
        return [pltpu.make_async_copy(src.at[stream, :, h, :], dst.at[s, h], sems.at[s, a, h])
                for a, (src, dst) in enumerate(((ck_hbm, k_buf), (cv_hbm, v_buf)))
                for h in range(n_heads)]

    @pl.when(b == 0)
    def _():
        for cp in head_copies(0, 0):
            cp.start()

    @pl.when(b + 1 < n_streams)
    def _():
        for cp in head_copies(b + 1, 1 - slot):
            cp.start()

    for cp in head_copies(b, slot):
        cp.wait()

    def past_rows(buf, head):
        return buf[slot, head].astype(BF16)

    ck_ref, cv_ref = k_buf, v_buf
    for g in range(n_heads // group):
        heads = range(g * group, (g + 1) * group)
        hsl = [slice(h * dh, (h + 1) * dh) for h in heads]
        z_past, z_new = [], []
        for h, hs in zip(heads, hsl):
            q = q_ref[:, hs]
            z_past.append(lax.dot_general(q, past_rows(ck_ref, h), NT_DIMS,
                                          preferred_element_type=F32))
            z_new.append(lax.dot_general(q, kn_ref[:, hs], NT_DIMS,
                                         preferred_element_type=F32) + bias_new)
        z_past = jnp.concatenate(z_past, axis=0)
        z_new = jnp.concatenate(z_new, axis=0)

        ls_new, lk_new = _sb_logs(z_new)
        w_new = jnp.exp2(ls_new + _sb_suffix(lk_new, u_new)).astype(BF16)
        off = jnp.sum(lk_new, axis=1, keepdims=True)
        ls_past, lk_past = _sb_logs(z_past)
        w_chunks = [None] * n_chunks
        for c in range(n_chunks - 1, -1, -1):
            cs = slice(c * chunk, (c + 1) * chunk)
            lk = lk_past[:, cs]
            w_chunks[c] = jnp.exp2(ls_past[:, cs] + _sb_suffix(lk, u_mat) + off).astype(BF16)
            off = off + jnp.sum(lk, axis=1, keepdims=True)
        w_past = jnp.concatenate(w_chunks, axis=1)

        for n, (h, hs) in enumerate(zip(heads, hsl)):
            rows = slice(n * t_len, (n + 1) * t_len)
            acc = (jnp.dot(w_past[rows], past_rows(cv_ref, h), preferred_element_type=F32)
                   + jnp.dot(w_new[rows], vn_ref[:, hs], preferred_element_type=F32))
            o_ref[:, hs] = (acc * za_ref[:, hs].astype(F32)).astype(o_ref.dtype)


def _attn_sample(q, ck, cv, kn, vn, za, *, row_offset, t_len, group, chunk):
    bsz, p_len, n_heads, dh = ck.shape
    width = q.shape[1]
    r = row_offset // t_len
    shifted = pl.BlockSpec((t_len, width), lambda b: (r + b, 0))
    local = pl.BlockSpec((t_len, width), lambda b: (b, 0))
    in_hbm = pl.BlockSpec(memory_space=pltpu.HBM)
    return pl.pallas_call(
        functools.partial(_attn_sample_body, group=group, chunk=chunk),
        grid=(bsz,),
        in_specs=[shifted, in_hbm, in_hbm, local, local, shifted],
        out_specs=local,
        out_shape=jax.ShapeDtypeStruct((bsz * t_len, width), BF16),
        scratch_shapes=[pltpu.VMEM((2, n_heads, p_len, dh), F32),
                        pltpu.VMEM((2, n_heads, p_len, dh), F32),
                        pltpu.SemaphoreType.DMA((2, 2, n_heads))],
        compiler_params=_params(1),
        name="attn_sample",
    )(q, ck, cv, kn, vn, za)


def _conv_body(u_ref, g_ref, st_ref, cw_ref, cb_ref, y_ref, ns_ref):
    u = u_ref[...]
    st = st_ref[0]
    t_len = u.shape[0]
    row = lax.broadcasted_iota(jnp.int32, u.shape, 0)
    u1 = jnp.where(row == 0, st[1:2, :], pltpu.roll(u, 1, axis=0))
    u2 = jnp.where(row == 0, st[0:1, :],
                   jnp.where(row == 1, st[1:2, :], pltpu.roll(u, 2, axis=0)))
    conv = cb_ref[...] + u2 * cw_ref[0:1, :] + u1 * cw_ref[1:2, :] + u * cw_ref[2:3, :]
    y_ref[...] = (g_ref[...].astype(F32) * conv).astype(y_ref.dtype)
    ns_ref[0] = u_ref[t_len - (CONV_WIDTH - 1):, :]


def _conv(u, gate, state, conv_w, conv_b, *, row_offset, t_len, tc):
    bsz = state.shape[0]
    ch = u.shape[1]
    tc = min(tc, ch)
    r = row_offset // t_len
    in_spec = pl.BlockSpec((t_len, tc), lambda b, c: (r + b, c))
    st_spec = pl.BlockSpec((1, CONV_WIDTH - 1, tc), lambda b, c: (b, 0, c))
    return pl.pallas_call(
        _conv_body,
        grid=(bsz, ch // tc),
        in_specs=[in_spec, in_spec, st_spec,
                  pl.BlockSpec((CONV_WIDTH, tc), lambda b, c: (0, c)),
                  pl.BlockSpec((1, tc), lambda b, c: (0, c))],
        out_specs=[pl.BlockSpec((t_len, tc), lambda b, c: (b, c)), st_spec],
        out_shape=[jax.ShapeDtypeStruct((bsz * t_len, ch), BF16),
                   jax.ShapeDtypeStruct((bsz, CONV_WIDTH - 1, ch), F32)],
        compiler_params=_params(2),
        name="short_conv",
    )(u, gate, state, conv_w, conv_b)


def _merge_body(ya_ref, yb_ref, wa_ref, wb_ref, ga_ref, gb_ref, o_ref):
    p_a = jnp.dot(ya_ref[...], wa_ref[...], preferred_element_type=F32)
    p_b = jnp.dot(yb_ref[...], wb_ref[...], preferred_element_type=F32)
    merged = ga_ref[...].astype(F32) * p_a + gb_ref[...].astype(F32) * p_b
    o_ref[...] = merged.astype(o_ref.dtype)


def _merge(y_a, y_b, w_a, w_b, gates, *, row_offset, tm, tn):
    m, ka = y_a.shape
    kb = y_b.shape[1]
    d = w_a.shape[1]
    tn = min(tn, d)
    nj = d // tn
    r = row_offset // tm
    return pl.pallas_call(
        _merge_body,
        grid=(m // tm, nj),
        in_specs=[pl.BlockSpec((tm, ka), lambda i, j: (i, 0)),
                  pl.BlockSpec((tm, kb), lambda i, j: (i, 0)),
                  pl.BlockSpec((ka, tn), lambda i, j: (0, j)),
                  pl.BlockSpec((kb, tn), lambda i, j: (0, j)),
                  pl.BlockSpec((tm, tn), lambda i, j: (r + i, j)),
                  pl.BlockSpec((tm, tn), lambda i, j: (r + i, nj + j))],
        out_specs=pl.BlockSpec((tm, tn), lambda i, j: (i, j)),
        out_shape=jax.ShapeDtypeStruct((m, d), BF16),
        compiler_params=_params(2),
        name="gated_merge",
    )(y_a, y_b, w_a, w_b, gates, gates)


def _out_body(m_ref, w_ref, x_ref, g_ref, b_ref, o_ref, *, alpha):
    pre = alpha * x_ref[...] + jnp.dot(m_ref[...], w_ref[...], preferred_element_type=F32)
    mu = jnp.mean(pre, axis=1, keepdims=True)
    cen = pre - mu
    var = jnp.mean(cen * cen, axis=1, keepdims=True)
    o_ref[...] = cen * lax.rsqrt(var + LN_EPS) * g_ref[...] + b_ref[...]


def _out_proj_ln(merged, w_o, x, ln_g, ln_b, *, alpha, tm):
    m, d = x.shape
    tm = min(tm, m)
    row_spec = lambda: pl.BlockSpec((tm, d), lambda i: (i, 0))
    const_spec = lambda r: pl.BlockSpec((r, d), lambda i: (0, 0), pipeline_mode=pl.Buffered(1))
    return pl.pallas_call(
        functools.partial(_out_body, alpha=alpha),
        grid=(m // tm,),
        in_specs=[row_spec(), const_spec(d), row_spec(), const_spec(1), const_spec(1)],
        out_specs=row_spec(),
        out_shape=jax.ShapeDtypeStruct((m, d), F32),
        compiler_params=_params(1, vmem_limit_bytes=OUT_PROJ_VMEM_LIMIT_BYTES),
        name="out_proj_ln",
    )(merged, w_o, x, ln_g, ln_b)


def _layer(x, past_k, past_v, past_conv, w_in, w_gates, b_in, conv_w, conv_b, w_a, w_b, w_o,
           ln_g, ln_b, rounded, *, n_heads, dh, alpha):
    bsz, t_len, d = x.shape
    m = bsz * t_len
    w_att = n_heads * dh
    w_conv = conv_w.shape[1]
    tm = min(ROW_TILE, m)
    x2d = x.reshape(m, d)
    q_scale = dh ** -0.5 * math.log2(math.e)

    o_q, o_k, o_v, o_za = 0, w_att, 2 * w_att, 3 * w_att
    o_bg = 4 * w_att
    o_cg, o_h, o_zb = o_bg + w_conv, o_bg + 2 * w_conv, o_bg + 3 * w_conv
    o_g = o_bg + 4 * w_conv
    ride = rounded is None
    gates_out = _proj(x2d, w_gates, b_in[:, o_g:], [0], 2 * d, lambda a: (_sigmoid(a),), [BF16],
                      tm=max(tm // 2, 8), tn=1024, name="proj_gates",
                      round_cols=(w_in, o_g) if ride else None)
    xb, gates = gates_out[0], gates_out[-1]
    w_main = gates_out[1] if ride else rounded[0]

    proj = functools.partial(_proj, xb, w_main, b_in, tm=tm)
    (q,) = proj([o_q], w_att, lambda a: (a * q_scale,), [BF16], tn=1024, name="proj_q")
    k, kb16 = _proj_heads(xb, w_main, b_in, o_k, w_att, dh, tm=tm, tn=1024, name="proj_k")
    v, vb16 = _proj_heads(xb, w_main, b_in, o_v, w_att, dh, tm=tm, tn=1024, name="proj_v")
    za_out = proj([o_za], w_att, lambda a: (_silu(a),), [BF16], tn=1024, name="proj_za",
                  round_cols=(w_o, w_o.shape[1]) if ride else None)
    gate_b_out = proj([o_bg, o_zb], w_conv, lambda bg, zb: (_silu(zb) * bg,), [BF16], tn=512,
                      name="proj_gate_b", round_cols=(w_b, w_b.shape[1]) if ride else None)
    u_out = proj([o_cg, o_h], w_conv, lambda cg, h: (cg * h,), [F32], tn=512, name="proj_u",
                 round_cols=(w_a, w_a.shape[1]) if ride else None)
    za, gate_b, u = za_out[-1], gate_b_out[-1], u_out[-1]
    if ride:
        rounded = (w_main, u_out[0], gate_b_out[0], za_out[0])
    _, w_a, w_b, w_o = rounded

    if past_k is None:
        y_a = _attn_prompt(q, kb16, vb16, za, bsz=bsz, t_len=t_len, dh=dh, heads_per_step=4,
                           blk=ATT_BLOCK)
        past_conv = jnp.zeros((bsz, CONV_WIDTH - 1, w_conv), F32)
        tc = 512
    else:
        y_a = _attn_sample(q, past_k, past_v, kb16, vb16, za, row_offset=0, t_len=t_len,
                           group=4, chunk=ATT_BLOCK)
        tc = w_conv
    y_b, new_conv = _conv(u, gate_b, past_conv, conv_w, conv_b, row_offset=0, t_len=t_len, tc=tc)
    merged = _merge(y_a, y_b, w_a, w_b, gates, row_offset=0, tm=tm, tn=1024)
    y = _out_proj_ln(merged, w_o, x2d, ln_g, ln_b, alpha=alpha, tm=256)
    return (y.reshape(bsz, t_len, d), k.reshape(bsz, t_len, n_heads, dh),
            v.reshape(bsz, t_len, n_heads, dh), new_conv), rounded


def kernel(x_prompt, x_sample, cache_k, cache_v, state_conv, w_in, b_in, conv_w, conv_b,
           w_a, w_b, w_o, ln_g, ln_b):
    depth = w_in.shape[0]
    n_heads, dh = cache_k.shape[-2:]
    alpha = (2.0 * depth) ** 0.25
    xp, xs = x_prompt, x_sample
    kp, vp, cp, kn, vn, cn = [], [], [], [], [], []
    for l in range(depth):
        n_gate_cols = 2 * w_o.shape[-1]
        wts = (w_in[l], w_in[l][:, -n_gate_cols:].astype(BF16), b_in[l][None, :],
               conv_w[l], conv_b[l][None, :], w_a[l], w_b[l], w_o[l],
               ln_g[l][None, :], ln_b[l][None, :])
        layer = functools.partial(_layer, n_heads=n_heads, dh=dh, alpha=alpha)
        (xp, k1, v1, c1), rounded = layer(xp, None, None, None, *wts, None)
        (xs, k2, v2, c2), _ = layer(xs, cache_k[l], cache_v[l], state_conv[l], *wts, rounded)
        kp.append(k1); vp.append(v1); cp.append(c1)
        kn.append(k2); vn.append(v2); cn.append(c2)
    return (xp, xs, jnp.stack(kp), jnp.stack(vp), jnp.stack(cp),
            jnp.stack(kn), jnp.stack(vn), jnp.stack(cn))
```

```python
import functools
import math

import jax
import jax.numpy as jnp
from jax import lax
from jax.experimental import pallas as pl
from jax.experimental.pallas import tpu as pltpu

F32 = jnp.float32
BF16 = jnp.bfloat16

LN_EPS = 1e-5
CONV_WIDTH = 3
VMEM_LIMIT_BYTES = 58 * 1024 * 1024
OUT_PROJ_VMEM_LIMIT_BYTES = 60 * 1024 * 1024
LANES = 128
ATT_BLOCK = 256
ROW_TILE = 1024
MASKED_SCORE = -1e30
UNDERFLOW_LOG2 = -160.0
NT_DIMS = (((1,), (1,)), ((), ()))


def _params(n_grid_dims, vmem_limit_bytes=VMEM_LIMIT_BYTES):
    return pltpu.CompilerParams(
        dimension_semantics=("arbitrary",) * n_grid_dims,
        vmem_limit_bytes=vmem_limit_bytes,
    )


def _sigmoid(x):
    return 1.0 / (1.0 + jnp.exp(-x))


def _silu(x):
    return x * _sigmoid(x)


def _proj_body(*refs, n_slabs, n_out, epilogue):
    x_ref = refs[0]
    w_refs = refs[1:1 + n_slabs]
    b_refs = refs[1 + n_slabs:1 + 2 * n_slabs]
    o_refs = refs[-n_out:]
    x = x_ref[...]
    accs = [jnp.dot(x, w[...], preferred_element_type=F32) + b[...]
            for w, b in zip(w_refs, b_refs)]
    for o_ref, val in zip(o_refs, epilogue(*accs)):
        o_ref[...] = val.astype(o_ref.dtype)
    if len(refs) > 1 + 2 * n_slabs + n_out:
        refs[1 + 2 * n_slabs + 1][...] = refs[1 + 2 * n_slabs][...].astype(BF16)


def _proj(x, w, b, col_offsets, width, epilogue, out_dtypes, *, tm, tn, name,
          round_cols=None):
    m, k = x.shape
    n_slabs = len(col_offsets)
    tn = min(tn, width)
    ni, nj = m // tm, width // tn
    in_specs = [pl.BlockSpec((tm, k), lambda i, j: (i, 0))]
    for off in col_offsets:
        in_specs.append(pl.BlockSpec((k, tn), lambda i, j, o=off // tn: (0, o + j)))
    for off in col_offsets:
        in_specs.append(pl.BlockSpec((1, tn), lambda i, j, o=off // tn: (0, o + j)))
    out_specs = [pl.BlockSpec((tm, tn), lambda i, j: (i, j)) for _ in out_dtypes]
    out_shape = [jax.ShapeDtypeStruct((m, width), dt) for dt in out_dtypes]
    operands = [x, *([w] * n_slabs), *([b] * n_slabs)]
    if round_cols is not None:
        src, n_cols = round_cols
        chunk, rem = divmod(n_cols, ni * nj)
        assert rem == 0 and chunk % LANES == 0, (n_cols, ni, nj)
        chunk_spec = pl.BlockSpec((src.shape[0], chunk), lambda i, j: (0, i * nj + j))
        in_specs.append(chunk_spec)
        out_specs.insert(0, chunk_spec)
        out_shape.insert(0, jax.ShapeDtypeStruct((src.shape[0], n_cols), BF16))
        operands.append(src)
    return pl.pallas_call(
        functools.partial(_proj_body, n_slabs=n_slabs, n_out=len(out_dtypes), epilogue=epilogue),
        grid=(ni, nj),
        in_specs=in_specs,
        out_specs=out_specs,
        out_shape=out_shape,
        compiler_params=_params(2),
        name=name,
    )(*operands)


def _proj_heads_body(x_ref, w_ref, b_ref, of_ref, ob_ref):
    acc = jnp.dot(x_ref[...], w_ref[...], preferred_element_type=F32) + b_ref[...]
    ob_ref[...] = acc.astype(ob_ref.dtype)
    of_ref[...] = acc.reshape(of_ref.shape)


def _proj_heads(x, w, b, col_offset, width, dh, *, tm, tn, name):
    m, k = x.shape
    tn = min(tn, width)
    o = col_offset // tn
    return pl.pallas_call(
        _proj_heads_body,
        grid=(m // tm, width // tn),
        in_specs=[pl.BlockSpec((tm, k), lambda i, j: (i, 0)),
                  pl.BlockSpec((k, tn), lambda i, j: (0, o + j)),
                  pl.BlockSpec((1, tn), lambda i, j: (0, o + j))],
        out_specs=[pl.BlockSpec((tm, tn // dh, dh), lambda i, j: (i, j, 0)),
                   pl.BlockSpec((tm, tn), lambda i, j: (i, j))],
        out_shape=[jax.ShapeDtypeStruct((m, width // dh, dh), F32),
                   jax.ShapeDtypeStruct((m, width), BF16)],
        compiler_params=_params(2),
        name=name,
    )(x, w, b)


def _suffix_ones(n):
    j = lax.broadcasted_iota(jnp.int32, (n, n), 0)
    s = lax.broadcasted_iota(jnp.int32, (n, n), 1)
    return (j > s).astype(BF16)


def _causal_bias(n):
    row = lax.broadcasted_iota(jnp.int32, (n, n), 0)
    col = lax.broadcasted_iota(jnp.int32, (n, n), 1)
    return jnp.where(col < row, 0.0, MASKED_SCORE).astype(F32)


def _sb_logs(z):
    ls = jnp.minimum(z, 0.0) - jnp.log2(1.0 + jnp.exp2(-jnp.abs(z)))
    return ls, ls - z


def _add_lane_tiled(a, r):
    return jnp.concatenate(
        [a[:, c:c + LANES] + r for c in range(0, a.shape[1], LANES)], axis=1)


def _sb_suffix(lk, u_mat):
    return jnp.dot(lk.astype(BF16), u_mat, preferred_element_type=F32)


def _attn_prompt_body(q_ref, k_ref, v_ref, za_ref, o_ref,
                      bias_ref, z_buf, tw_buf, af_buf, run_ref, acc_ref, *, heads, dh, blk):
    t_len = q_ref.shape[0]
    nq = t_len // blk
    u_mat = _suffix_ones(blk)
    bias_ref[0:blk, :] = _causal_bias(blk)
    bias_ref[blk:2 * blk, :] = jnp.zeros((blk, blk), F32)
    hsl = [slice(h * dh, (h + 1) * dh) for h in range(heads)]
    for h in range(heads):
        z_buf[1, h] = jnp.full((blk, blk), MASKED_SCORE, F32)
        tw_buf[1, h] = jnp.full((blk, blk), MASKED_SCORE, F32)
        af_buf[1, h] = jnp.zeros((blk, blk), F32)
        run_ref[h] = jnp.zeros((blk, LANES), F32)
        acc_ref[h] = jnp.zeros((blk, dh), F32)

    def step(st, new):
        (qa, ka), (qb, kb, last_b, valid_b), (qc, kc, last_c, valid_c) = st
        old = 1 - new
        valid_a = qa < nq
        q0 = pl.multiple_of(jnp.minimum(qa, nq - 1) * blk, blk)
        k0 = pl.multiple_of(jnp.minimum(ka, nq - 1) * blk, blk)
        b0 = pl.multiple_of(jnp.where(ka == qa, 0, blk), blk)
        bias = bias_ref[pl.ds(b0, blk), :]
        z_new = [lax.dot_general(q_ref[pl.ds(q0, blk), hsl[h]],
                                 k_ref[pl.ds(k0, blk), hsl[h]], NT_DIMS,
                                 preferred_element_type=F32) for h in range(heads)]
        first = kb == qb
        af_new = []
        run_max = None
        for h in range(heads):
            ls, lk = _sb_logs(z_buf[old, h])
            off = jnp.where(first, 0.0, run_ref[h])
            off = jnp.where(valid_b == 1, off, MASKED_SCORE)
            tw_buf[new, h] = _add_lane_tiled(ls, off)
            af_new.append(_sb_suffix(lk, u_mat))
            run = off + jnp.broadcast_to(jnp.sum(lk, axis=1, keepdims=True), off.shape)
            run_ref[h] = run
            run_max = jnp.max(run) if run_max is None else jnp.maximum(run_max, jnp.max(run))
        vc0 = pl.multiple_of(kc * blk, blk)
        qc0 = pl.multiple_of(qc * blk, blk)
        restart = jnp.logical_and(last_c == 1, qc < nq - 1)
        for h in range(heads):
            w = jnp.exp2(tw_buf[old, h] + af_buf[old, h]).astype(BF16)
            acc = acc_ref[h] + jnp.dot(w, v_ref[pl.ds(vc0, blk), hsl[h]],
                                       preferred_element_type=F32)
            gate = za_ref[pl.ds(qc0, blk), hsl[h]].astype(F32)
            o_ref[pl.ds(qc0, blk), hsl[h]] = (acc * gate).astype(o_ref.dtype)
            acc_ref[h] = jnp.where(restart, 0.0, acc)
        for h in range(heads):
            z_buf[new, h] = z_new[h] + bias
            af_buf[new, h] = af_new[h]
        dead = jnp.logical_and(valid_b == 1, run_max < UNDERFLOW_LOG2)
        ends = jnp.logical_or(ka == 0, jnp.logical_and(dead, qa == qb))
        nxt_q = jnp.where(jnp.logical_and(ends, valid_a), qa + 1, qa)
        nxt_k = jnp.where(ends, nxt_q, ka - 1)
        return ((nxt_q, nxt_k),
                (jnp.minimum(qa, nq - 1), jnp.minimum(ka, nq - 1), ends.astype(jnp.int32),
                 valid_a.astype(jnp.int32)),
                (qb, kb, last_b, valid_b))

    def in_flight(st):
        (qa, _), (_, _, _, valid_b), (_, _, _, valid_c) = st
        return jnp.logical_or(qa < nq, jnp.logical_or(valid_b == 1, valid_c == 1))

    zero = jnp.int32(0)
    lax.while_loop(in_flight, lambda st: step(step(st, 0), 1),
                   ((zero, zero), (zero, zero, zero, zero), (zero, zero, zero, zero)))


def _attn_prompt(q, k, v, za, *, bsz, t_len, dh, heads_per_step, blk):
    width = q.shape[1]
    hw = heads_per_step * dh
    spec = pl.BlockSpec((t_len, hw), lambda b, g: (b, g))
    return pl.pallas_call(
        functools.partial(_attn_prompt_body, heads=heads_per_step, dh=dh, blk=blk),
        grid=(bsz, width // hw),
        in_specs=[spec, spec, spec, spec],
        out_specs=spec,
        out_shape=jax.ShapeDtypeStruct((bsz * t_len, width), BF16),
        scratch_shapes=[pltpu.VMEM((2 * blk, blk), F32),
                        pltpu.VMEM((2, heads_per_step, blk, blk), F32),
                        pltpu.VMEM((2, heads_per_step, blk, blk), F32),
                        pltpu.VMEM((2, heads_per_step, blk, blk), F32),
                        pltpu.VMEM((heads_per_step, blk, LANES), F32),
                        pltpu.VMEM((heads_per_step, blk, dh), F32)],
        compiler_params=_params(2),
        name="attn_prompt",
    )(q, k, v, za)


def _attn_sample_body(q_ref, ck_hbm, cv_hbm, kn_ref, vn_ref, za_ref, o_ref, k_buf, v_buf, sems,
                      *, group, chunk):
    b = pl.program_id(0)
    n_streams = pl.num_programs(0)
    t_len = q_ref.shape[0]
    _, n_heads, p_len, dh = k_buf.shape
    n_chunks = p_len // chunk
    u_mat = _suffix_ones(chunk)
    u_new = _suffix_ones(t_len)
    bias_new = _causal_bias(t_len)
    slot = lax.rem(b, 2)

    def head_copies(stream, s):
        return [pltpu.make_async_copy(src.at[stream, :, h, :], dst.at[s, h], sems.at[s, a, h])
                for a, (src, dst) in enumerate(((ck_hbm, k_buf), (cv_hbm, v_buf)))
                for h in range(n_heads)]

    @pl.when(b == 0)
    def _():
        for cp in head_copies(0, 0):
            cp.start()

    @pl.when(b + 1 < n_streams)
    def _():
        for cp in head_copies(b + 1, 1 - slot):
            cp.start()

    for cp in head_copies(b, slot):
        cp.wait()

    def past_rows(buf, head):
        return buf[slot, head].astype(BF16)

    ck_ref, cv_ref = k_buf, v_buf
    for g in range(n_heads // group):
        heads = range(g * group, (g + 1) * group)
        hsl = [slice(h * dh, (h + 1) * dh) for h in heads]
        z_past, z_new = [], []
        for h, hs in zip(heads, hsl):
            q = q_ref[:, hs]
            z_past.append(lax.dot_general(q, past_rows(ck_ref, h), NT_DIMS,
                                          preferred_element_type=F32))
            z_new.append(lax.dot_general(q, kn_ref[:, hs], NT_DIMS,
                                         preferred_element_type=F32) + bias_new)
        z_past = jnp.concatenate(z_past, axis=0)
        z_new = jnp.concatenate(z_new, axis=0)

        ls_new, lk_new = _sb_logs(z_new)
        w_new = jnp.exp2(ls_new + _sb_suffix(lk_new, u_new)).astype(BF16)
        off = jnp.sum(lk_new, axis=1, keepdims=True)
        ls_past, lk_past = _sb_logs(z_past)
        w_chunks = [None] * n_chunks
        for c in range(n_chunks - 1, -1, -1):
            cs = slice(c * chunk, (c + 1) * chunk)
            lk = lk_past[:, cs]
            w_chunks[c] = jnp.exp2(ls_past[:, cs] + _sb_suffix(lk, u_mat) + off).astype(BF16)
            off = off + jnp.sum(lk, axis=1, keepdims=True)
        w_past = jnp.concatenate(w_chunks, axis=1)

        for n, (h, hs) in enumerate(zip(heads, hsl)):
            rows = slice(n * t_len, (n + 1) * t_len)
            acc = (jnp.dot(w_past[rows], past_rows(cv_ref, h), preferred_element_type=F32)
                   + jnp.dot(w_new[rows], vn_ref[:, hs], preferred_element_type=F32))
            o_ref[:, hs] = (acc * za_ref[:, hs].astype(F32)).astype(o_ref.dtype)


def _attn_sample(q, ck, cv, kn, vn, za, *, row_offset, t_len, group, chunk):
    bsz, p_len, n_heads, dh = ck.shape
    width = q.shape[1]
    r = row_offset // t_len
    shifted = pl.BlockSpec((t_len, width), lambda b: (r + b, 0))
    local = pl.BlockSpec((t_len, width), lambda b: (b, 0))
    in_hbm = pl.BlockSpec(memory_space=pltpu.HBM)
    return pl.pallas_call(
        functools.partial(_attn_sample_body, group=group, chunk=chunk),
        grid=(bsz,),
        in_specs=[shifted, in_hbm, in_hbm, local, local, shifted],
        out_specs=local,
        out_shape=jax.ShapeDtypeStruct((bsz * t_len, width), BF16),
        scratch_shapes=[pltpu.VMEM((2, n_heads, p_len, dh), F32),
                        pltpu.VMEM((2, n_heads, p_len, dh), F32),
                        pltpu.SemaphoreType.DMA((2, 2, n_heads))],
        compiler_params=_params(1),
        name="attn_sample",
    )(q, ck, cv, kn, vn, za)


def _conv_body(u_ref, g_ref, st_ref, cw_ref, cb_ref, y_ref, ns_ref):
    u = u_ref[...]
    st = st_ref[0]
    t_len = u.shape[0]
    row = lax.broadcasted_iota(jnp.int32, u.shape, 0)
    u1 = jnp.where(row == 0, st[1:2, :], pltpu.roll(u, 1, axis=0))
    u2 = jnp.where(row == 0, st[0:1, :],
                   jnp.where(row == 1, st[1:2, :], pltpu.roll(u, 2, axis=0)))
    conv = cb_ref[...] + u2 * cw_ref[0:1, :] + u1 * cw_ref[1:2, :] + u * cw_ref[2:3, :]
    y_ref[...] = (g_ref[...].astype(F32) * conv).astype(y_ref.dtype)
    ns_ref[0] = u_ref[t_len - (CONV_WIDTH - 1):, :]


def _conv(u, gate, state, conv_w, conv_b, *, row_offset, t_len, tc):
    bsz = state.shape[0]
    ch = u.shape[1]
    tc = min(tc, ch)
    r = row_offset // t_len
    in_spec = pl.BlockSpec((t_len, tc), lambda b, c: (r + b, c))
    st_spec = pl.BlockSpec((1, CONV_WIDTH - 1, tc), lambda b, c: (b, 0, c))
    return pl.pallas_call(
        _conv_body,
        grid=(bsz, ch // tc),
        in_specs=[in_spec, in_spec, st_spec,
                  pl.BlockSpec((CONV_WIDTH, tc), lambda b, c: (0, c)),
                  pl.BlockSpec((1, tc), lambda b, c: (0, c))],
        out_specs=[pl.BlockSpec((t_len, tc), lambda b, c: (b, c)), st_spec],
        out_shape=[jax.ShapeDtypeStruct((bsz * t_len, ch), BF16),
                   jax.ShapeDtypeStruct((bsz, CONV_WIDTH - 1, ch), F32)],
        compiler_params=_params(2),
        name="short_conv",
    )(u, gate, state, conv_w, conv_b)


def _merge_body(ya_ref, yb_ref, wa_ref, wb_ref, ga_ref, gb_ref, o_ref):
    p_a = jnp.dot(ya_ref[...], wa_ref[...], preferred_element_type=F32)
    p_b = jnp.dot(yb_ref[...], wb_ref[...], preferred_element_type=F32)
    merged = ga_ref[...].astype(F32) * p_a + gb_ref[...].astype(F32) * p_b
    o_ref[...] = merged.astype(o_ref.dtype)


def _merge(y_a, y_b, w_a, w_b, gates, *, row_offset, tm, tn):
    m, ka = y_a.shape
    kb = y_b.shape[1]
    d = w_a.shape[1]
    tn = min(tn, d)
    nj = d // tn
    r = row_offset // tm
    return pl.pallas_call(
        _merge_body,
        grid=(m // tm, nj),
        in_specs=[pl.BlockSpec((tm, ka), lambda i, j: (i, 0)),
                  pl.BlockSpec((tm, kb), lambda i, j: (i, 0)),
                  pl.BlockSpec((ka, tn), lambda i, j: (0, j)),
                  pl.BlockSpec((kb, tn), lambda i, j: (0, j)),
                  pl.BlockSpec((tm, tn), lambda i, j: (r + i, j)),
                  pl.BlockSpec((tm, tn), lambda i, j: (r + i, nj + j))],
        out_specs=pl.BlockSpec((tm, tn), lambda i, j: (i, j)),
        out_shape=jax.ShapeDtypeStruct((m, d), BF16),
        compiler_params=_params(2),
        name="gated_merge",
    )(y_a, y_b, w_a, w_b, gates, gates)


def _out_body(m_ref, w_ref, x_ref, g_ref, b_ref, o_ref, *, alpha):
    pre = alpha * x_ref[...] + jnp.dot(m_ref[...], w_ref[...], preferred_element_type=F32)
    mu = jnp.mean(pre, axis=1, keepdims=True)
    cen = pre - mu
    var = jnp.mean(cen * cen, axis=1, keepdims=True)
    o_ref[...] = cen * lax.rsqrt(var + LN_EPS) * g_ref[...] + b_ref[...]


def _out_proj_ln(merged, w_o, x, ln_g, ln_b, *, alpha, tm):
    m, d = x.shape
    tm = min(tm, m)
    row_spec = lambda: pl.BlockSpec((tm, d), lambda i: (i, 0))
    const_spec = lambda r: pl.BlockSpec((r, d), lambda i: (0, 0), pipeline_mode=pl.Buffered(1))
    return pl.pallas_call(
        functools.partial(_out_body, alpha=alpha),
        grid=(m // tm,),
        in_specs=[row_spec(), const_spec(d), row_spec(), const_spec(1), const_spec(1)],
        out_specs=row_spec(),
        out_shape=jax.ShapeDtypeStruct((m, d), F32),
        compiler_params=_params(1, vmem_limit_bytes=OUT_PROJ_VMEM_LIMIT_BYTES),
        name="out_proj_ln",
    )(merged, w_o, x, ln_g, ln_b)


def _layer(x, past_k, past_v, past_conv, w_in, w_gates, b_in, conv_w, conv_b, w_a, w_b, w_o,
           ln_g, ln_b, rounded, *, n_heads, dh, alpha):
    bsz, t_len, d = x.shape
    m = bsz * t_len
    w_att = n_heads * dh
    w_conv = conv_w.shape[1]
    tm = min(ROW_TILE, m)
    x2d = x.reshape(m, d)
    q_scale = dh ** -0.5 * math.log2(math.e)

    o_q, o_k, o_v, o_za = 0, w_att, 2 * w_att, 3 * w_att
    o_bg = 4 * w_att
    o_cg, o_h, o_zb = o_bg + w_conv, o_bg + 2 * w_conv, o_bg + 3 * w_conv
    o_g = o_bg + 4 * w_conv
    ride = rounded is None
    xb = x2d.astype(BF16)
    gates_out = _proj(xb, w_gates, b_in[:, o_g:], [0], 2 * d, lambda a: (_sigmoid(a),), [BF16],
                      tm=tm, tn=1024, name="proj_gates",
                      round_cols=(w_in, o_g) if ride else None)
    gates = gates_out[-1]
    w_main = gates_out[0] if ride else rounded[0]

    proj = functools.partial(_proj, xb, w_main, b_in, tm=tm)
    (q,) = proj([o_q], w_att, lambda a: (a * q_scale,), [BF16], tn=1024, name="proj_q")
    k, kb16 = _proj_heads(xb, w_main, b_in, o_k, w_att, dh, tm=tm, tn=1024, name="proj_k")
    v, vb16 = _proj_heads(xb, w_main, b_in, o_v, w_att, dh, tm=tm, tn=1024, name="proj_v")
    za_out = proj([o_za], w_att, lambda a: (_silu(a),), [BF16], tn=1024, name="proj_za",
                  round_cols=(w_o, w_o.shape[1]) if ride else None)
    gate_b_out = proj([o_bg, o_zb], w_conv, lambda bg, zb: (_silu(zb) * bg,), [BF16], tn=512,
                      name="proj_gate_b", round_cols=(w_b, w_b.shape[1]) if ride else None)
    u_out = proj([o_cg, o_h], w_conv, lambda cg, h: (cg * h,), [F32], tn=512, name="proj_u",
                 round_cols=(w_a, w_a.shape[1]) if ride else None)
    za, gate_b, u = za_out[-1], gate_b_out[-1], u_out[-1]
    if ride:
        rounded = (w_main, u_out[0], gate_b_out[0], za_out[0])
    _, w_a, w_b, w_o = rounded

    if past_k is None:
        y_a = _attn_prompt(q, kb16, vb16, za, bsz=bsz, t_len=t_len, dh=dh, heads_per_step=4,
                           blk=ATT_BLOCK)
        past_conv = jnp.zeros((bsz, CONV_WIDTH - 1, w_conv), F32)
        tc = 512
    else:
        y_a = _attn_sample(q, past_k, past_v, kb16, vb16, za, row_offset=0, t_len=t_len,
                           group=4, chunk=ATT_BLOCK)
        tc = w_conv
    y_b, new_conv = _conv(u, gate_b, past_conv, conv_w, conv_b, row_offset=0, t_len=t_len, tc=tc)
    merged = _merge(y_a, y_b, w_a, w_b, gates, row_offset=0, tm=tm, tn=1024)
    y = _out_proj_ln(merged, w_o, x2d, ln_g, ln_b, alpha=alpha, tm=256)
    return (y.reshape(bsz, t_len, d), k.reshape(bsz, t_len, n_heads, dh),
            v.reshape(bsz, t_len, n_heads, dh), new_conv), rounded


def kernel(x_prompt, x_sample, cache_k, cache_v, state_conv, w_in, b_in, conv_w, conv_b,
           w_a, w_b, w_o, ln_g, ln_b):
    depth = w_in.shape[0]
    n_heads, dh = cache_k.shape[-2:]
    alpha = (2.0 * depth) ** 0.25
    xp, xs = x_prompt, x_sample
    kp, vp, cp, kn, vn, cn = [], [], [], [], [], []
    for l in range(depth):
        n_gate_cols = 2 * w_o.shape[-1]
        wts = (w_in[l], w_in[l][:, -n_gate_cols:].astype(BF16), b_in[l][None, :],
               conv_w[l], conv_b[l][None, :], w_a[l], w_b[l], w_o[l],
               ln_g[l][None, :], ln_b[l][None, :])
        layer = functools.partial(_layer, n_heads=n_heads, dh=dh, alpha=alpha)
        (xp, k1, v1, c1), rounded = layer(xp, None, None, None, *wts, None)
        (xs, k2, v2, c2), _ = layer(xs, cache_k[l], cache_v[l], state_conv[l], *wts, rounded)
        kp.append(k1); vp.append(v1); cp.append(c1)
        kn.append(k2); vn.append(v2); cn.append(c2)
    return (xp, xs, jnp.stack(kp), jnp.stack(vp), jnp.stack(cp),
            jnp.stack(kn), jnp.stack(vn), jnp.stack(cn))
```

```python
import functools
import math

import jax
import jax.numpy as jnp
from jax import lax
from jax.experimental import pallas as pl
from jax.experimental.pallas import tpu as pltpu

F32 = jnp.float32
BF16 = jnp.bfloat16

LN_EPS = 1e-5
CONV_WIDTH = 3
VMEM_LIMIT_BYTES = 58 * 1024 * 1024
OUT_PROJ_VMEM_LIMIT_BYTES = 60 * 1024 * 1024
LANES = 128
ATT_BLOCK = 256
ROW_TILE = 1024
MASKED_SCORE = -1e30
UNDERFLOW_LOG2 = -160.0
NT_DIMS = (((1,), (1,)), ((), ()))


def _params(n_grid_dims, vmem_limit_bytes=VMEM_LIMIT_BYTES):
    return pltpu.CompilerParams(
        dimension_semantics=("arbitrary",) * n_grid_dims,
        vmem_limit_bytes=vmem_limit_bytes,
    )


def _sigmoid(x):
    return 1.0 / (1.0 + jnp.exp(-x))


def _silu(x):
    return x * _sigmoid(x)


def _proj_body(*refs, n_slabs, n_out, epilogue):
    x_ref = refs[0]
    w_refs = refs[1:1 + n_slabs]
    b_refs = refs[1 + n_slabs:1 + 2 * n_slabs]
    o_refs = refs[-n_out:]
    x = x_ref[...]
    accs = [jnp.dot(x, w[...], preferred_element_type=F32) + b[...]
            for w, b in zip(w_refs, b_refs)]
    for o_ref, val in zip(o_refs, epilogue(*accs)):
        o_ref[...] = val.astype(o_ref.dtype)
    if len(refs) > 1 + 2 * n_slabs + n_out:
        refs[1 + 2 * n_slabs + 1][...] = refs[1 + 2 * n_slabs][...].astype(BF16)


def _proj(x, w, b, col_offsets, width, epilogue, out_dtypes, *, tm, tn, name,
          round_cols=None):
    m, k = x.shape
    n_slabs = len(col_offsets)
    tn = min(tn, width)
    ni, nj = m // tm, width // tn
    in_specs = [pl.BlockSpec((tm, k), lambda i, j: (i, 0))]
    for off in col_offsets:
        in_specs.append(pl.BlockSpec((k, tn), lambda i, j, o=off // tn: (0, o + j)))
    for off in col_offsets:
        in_specs.append(pl.BlockSpec((1, tn), lambda i, j, o=off // tn: (0, o + j)))
    out_specs = [pl.BlockSpec((tm, tn), lambda i, j: (i, j)) for _ in out_dtypes]
    out_shape = [jax.ShapeDtypeStruct((m, width), dt) for dt in out_dtypes]
    operands = [x, *([w] * n_slabs), *([b] * n_slabs)]
    if round_cols is not None:
        src, n_cols = round_cols
        chunk, rem = divmod(n_cols, ni * nj)
        assert rem == 0 and chunk % LANES == 0, (n_cols, ni, nj)
        chunk_spec = pl.BlockSpec((src.shape[0], chunk), lambda i, j: (0, i * nj + j))
        in_specs.append(chunk_spec)
        out_specs.insert(0, chunk_spec)
        out_shape.insert(0, jax.ShapeDtypeStruct((src.shape[0], n_cols), BF16))
        operands.append(src)
    return pl.pallas_call(
        functools.partial(_proj_body, n_slabs=n_slabs, n_out=len(out_dtypes), epilogue=epilogue),
        grid=(ni, nj),
        in_specs=in_specs,
        out_specs=out_specs,
        out_shape=out_shape,
        compiler_params=_params(2),
        name=name,
    )(*operands)


def _proj_heads_body(x_ref, w_ref, b_ref, of_ref, ob_ref):
    acc = jnp.dot(x_ref[...], w_ref[...], preferred_element_type=F32) + b_ref[...]
    ob_ref[...] = acc.astype(ob_ref.dtype)
    of_ref[...] = acc.reshape(of_ref.shape)


def _proj_heads(x, w, b, col_offset, width, dh, *, tm, tn, name):
    m, k = x.shape
    tn = min(tn, width)
    o = col_offset // tn
    return pl.pallas_call(
        _proj_heads_body,
        grid=(m // tm, width // tn),
        in_specs=[pl.BlockSpec((tm, k), lambda i, j: (i, 0)),
                  pl.BlockSpec((k, tn), lambda i, j: (0, o + j)),
                  pl.BlockSpec((1, tn), lambda i, j: (0, o + j))],
        out_specs=[pl.BlockSpec((tm, tn // dh, dh), lambda i, j: (i, j, 0)),
                   pl.BlockSpec((tm, tn), lambda i, j: (i, j))],
        out_shape=[jax.ShapeDtypeStruct((m, width // dh, dh), F32),
                   jax.ShapeDtypeStruct((m, width), BF16)],
        compiler_params=_params(2),
        name=name,
    )(x, w, b)


def _suffix_ones(n):
    j = lax.broadcasted_iota(jnp.int32, (n, n), 0)
    s = lax.broadcasted_iota(jnp.int32, (n, n), 1)
    return (j > s).astype(BF16)


def _causal_bias(n):
    row = lax.broadcasted_iota(jnp.int32, (n, n), 0)
    col = lax.broadcasted_iota(jnp.int32, (n, n), 1)
    return jnp.where(col < row, 0.0, MASKED_SCORE).astype(F32)


def _sb_logs(z):
    ls = jnp.minimum(z, 0.0) - jnp.log2(1.0 + jnp.exp2(-jnp.abs(z)))
    return ls, ls - z


def _add_lane_tiled(a, r):
    return jnp.concatenate(
        [a[:, c:c + LANES] + r for c in range(0, a.shape[1], LANES)], axis=1)


def _sb_suffix(lk, u_mat):
    return jnp.dot(lk.astype(BF16), u_mat, preferred_element_type=F32)


def _attn_prompt_body(q_ref, k_ref, v_ref, za_ref, o_ref,
                      bias_ref, z_buf, tw_buf, af_buf, run_ref, acc_ref, *, heads, dh, blk):
    t_len = q_ref.shape[0]
    nq = t_len // blk
    u_mat = _suffix_ones(blk)
    bias_ref[0:blk, :] = _causal_bias(blk)
    bias_ref[blk:2 * blk, :] = jnp.zeros((blk, blk), F32)
    hsl = [slice(h * dh, (h + 1) * dh) for h in range(heads)]
    for h in range(heads):
        z_buf[1, h] = jnp.full((blk, blk), MASKED_SCORE, F32)
        tw_buf[1, h] = jnp.full((blk, blk), MASKED_SCORE, F32)
        af_buf[1, h] = jnp.zeros((blk, blk), F32)
        run_ref[h] = jnp.zeros((blk, LANES), F32)
        acc_ref[h] = jnp.zeros((blk, dh), F32)

    def step(st, new):
        (qa, ka), (qb, kb, last_b, valid_b), (qc, kc, last_c, valid_c) = st
        old = 1 - new
        first = kb == qb
        af_new = []
        run_max = None
        for h in range(heads):
            ls, lk = _sb_logs(z_buf[old, h])
            off = jnp.where(first, 0.0, run_ref[h])
            off = jnp.where(valid_b == 1, off, MASKED_SCORE)
            tw_buf[new, h] = _add_lane_tiled(ls, off)
            af_new.append(_sb_suffix(lk, u_mat))
            run = off + jnp.broadcast_to(jnp.sum(lk, axis=1, keepdims=True), off.shape)
            run_ref[h] = run
            run_max = jnp.max(run) if run_max is None else jnp.maximum(run_max, jnp.max(run))
        dead = jnp.logical_and(valid_b == 1, run_max < UNDERFLOW_LOG2)
        cut = jnp.logical_and(dead, jnp.logical_and(qa == qb, qa < nq))
        qs = jnp.where(cut, qa + 1, qa)
        ks = jnp.where(cut, qa + 1, ka)
        last_b = jnp.maximum(last_b, cut.astype(jnp.int32))
        valid_s = qs < nq
        q0 = pl.multiple_of(jnp.minimum(qs, nq - 1) * blk, blk)
        k0 = pl.multiple_of(jnp.minimum(ks, nq - 1) * blk, blk)
        b0 = pl.multiple_of(jnp.where(ks == qs, 0, blk), blk)
        bias = bias_ref[pl.ds(b0, blk), :]
        z_new = [lax.dot_general(q_ref[pl.ds(q0, blk), hsl[h]],
                                 k_ref[pl.ds(k0, blk), hsl[h]], NT_DIMS,
                                 preferred_element_type=F32) for h in range(heads)]
        vc0 = pl.multiple_of(kc * blk, blk)
        qc0 = pl.multiple_of(qc * blk, blk)
        restart = jnp.logical_and(last_c == 1, qc < nq - 1)
        for h in range(heads):
            w = jnp.exp2(tw_buf[old, h] + af_buf[old, h]).astype(BF16)
            acc = acc_ref[h] + jnp.dot(w, v_ref[pl.ds(vc0, blk), hsl[h]],
                                       preferred_element_type=F32)
            gate = za_ref[pl.ds(qc0, blk), hsl[h]].astype(F32)
            o_ref[pl.ds(qc0, blk), hsl[h]] = (acc * gate).astype(o_ref.dtype)
            acc_ref[h] = jnp.where(restart, 0.0, acc)
        for h in range(heads):
            z_buf[new, h] = z_new[h] + bias
            af_buf[new, h] = af_new[h]
        ends = ks == 0
        nxt_q = jnp.where(jnp.logical_and(ends, valid_s), qs + 1, qs)
        nxt_k = jnp.where(ends, nxt_q, ks - 1)
        return ((nxt_q, nxt_k),
                (jnp.minimum(qs, nq - 1), jnp.minimum(ks, nq - 1), ends.astype(jnp.int32),
                 valid_s.astype(jnp.int32)),
                (qb, kb, last_b, valid_b))

    def in_flight(st):
        (qa, _), (_, _, _, valid_b), (_, _, _, valid_c) = st
        return jnp.logical_or(qa < nq, jnp.logical_or(valid_b == 1, valid_c == 1))

    zero = jnp.int32(0)
    lax.while_loop(in_flight, lambda st: step(step(st, 0), 1),
                   ((zero, zero), (zero, zero, zero, zero), (zero, zero, zero, zero)))


def _attn_prompt(q, k, v, za, *, bsz, t_len, dh, heads_per_step, blk):
    width = q.shape[1]
    hw = heads_per_step * dh
    spec = pl.BlockSpec((t_len, hw), lambda b, g: (b, g))
    return pl.pallas_call(
        functools.partial(_attn_prompt_body, heads=heads_per_step, dh=dh, blk=blk),
        grid=(bsz, width // hw),
        in_specs=[spec, spec, spec, spec],
        out_specs=spec,
        out_shape=jax.ShapeDtypeStruct((bsz * t_len, width), BF16),
        scratch_shapes=[pltpu.VMEM((2 * blk, blk), F32),
                        pltpu.VMEM((2, heads_per_step, blk, blk), F32),
                        pltpu.VMEM((2, heads_per_step, blk, blk), F32),
                        pltpu.VMEM((2, heads_per_step, blk, blk), F32),
                        pltpu.VMEM((heads_per_step, blk, LANES), F32),
                        pltpu.VMEM((heads_per_step, blk, dh), F32)],
        compiler_params=_params(2),
        name="attn_prompt",
    )(q, k, v, za)


def _attn_sample_body(q_ref, ck_hbm, cv_hbm, kn_ref, vn_ref, za_ref, o_ref, k_buf, v_buf, sems,
                      *, group, chunk):
    b = pl.program_id(0)
    n_streams = pl.num_programs(0)
    t_len = q_ref.shape[0]
    _, n_heads, p_len, dh = k_buf.shape
    n_chunks = p_len // chunk
    u_mat = _suffix_ones(chunk)
    u_new = _suffix_ones(t_len)
    bias_new = _causal_bias(t_len)
    slot = lax.rem(b, 2)

    def head_copies(stream, s):
        return [pltpu.make_async_copy(src.at[stream, :, h, :], dst.at[s, h], sems.at[s, a, h])
                for a, (src, dst) in enumerate(((ck_hbm, k_buf), (cv_hbm, v_buf)))
                for h in range(n_heads)]

    @pl.when(b == 0)
    def _():
        for cp in head_copies(0, 0):
            cp.start()

    @pl.when(b + 1 < n_streams)
    def _():
        for cp in head_copies(b + 1, 1 - slot):
            cp.start()

    for cp in head_copies(b, slot):
        cp.wait()

    def past_rows(buf, head):
        return buf[slot, head].astype(BF16)

    ck_ref, cv_ref = k_buf, v_buf
    for g in range(n_heads // group):
        heads = range(g * group, (g + 1) * group)
        hsl = [slice(h * dh, (h + 1) * dh) for h in heads]
        z_past, z_new = [], []
        for h, hs in zip(heads, hsl):
            q = q_ref[:, hs]
            z_past.append(lax.dot_general(q, past_rows(ck_ref, h), NT_DIMS,
                                          preferred_element_type=F32))
            z_new.append(lax.dot_general(q, kn_ref[:, hs], NT_DIMS,
                                         preferred_element_type=F32) + bias_new)
        z_past = jnp.concatenate(z_past, axis=0)
        z_new = jnp.concatenate(z_new, axis=0)

        ls_new, lk_new = _sb_logs(z_new)
        w_new = jnp.exp2(ls_new + _sb_suffix(lk_new, u_new)).astype(BF16)
        off = jnp.sum(lk_new, axis=1, keepdims=True)
        ls_past, lk_past = _sb_logs(z_past)
        w_chunks = [None] * n_chunks
        for c in range(n_chunks - 1, -1, -1):
            cs = slice(c * chunk, (c + 1) * chunk)
            lk = lk_past[:, cs]
            w_chunks[c] = jnp.exp2(ls_past[:, cs] + _sb_suffix(lk, u_mat) + off).astype(BF16)
            off = off + jnp.sum(lk, axis=1, keepdims=True)
        w_past = jnp.concatenate(w_chunks, axis=1)

        for n, (h, hs) in enumerate(zip(heads, hsl)):
            rows = slice(n * t_len, (n + 1) * t_len)
            acc = (jnp.dot(w_past[rows], past_rows(cv_ref, h), preferred_element_type=F32)
                   + jnp.dot(w_new[rows], vn_ref[:, hs], preferred_element_type=F32))
            o_ref[:, hs] = (acc * za_ref[:, hs].astype(F32)).astype(o_ref.dtype)


def _attn_sample(q, ck, cv, kn, vn, za, *, row_offset, t_len, group, chunk):
    bsz, p_len, n_heads, dh = ck.shape
    width = q.shape[1]
    r = row_offset // t_len
    shifted = pl.BlockSpec((t_len, width), lambda b: (r + b, 0))
    local = pl.BlockSpec((t_len, width), lambda b: (b, 0))
    in_hbm = pl.BlockSpec(memory_space=pltpu.HBM)
    return pl.pallas_call(
        functools.partial(_attn_sample_body, group=group, chunk=chunk),
        grid=(bsz,),
        in_specs=[shifted, in_hbm, in_hbm, local, local, shifted],
        out_specs=local,
        out_shape=jax.ShapeDtypeStruct((bsz * t_len, width), BF16),
        scratch_shapes=[pltpu.VMEM((2, n_heads, p_len, dh), F32),
                        pltpu.VMEM((2, n_heads, p_len, dh), F32),
                        pltpu.SemaphoreType.DMA((2, 2, n_heads))],
        compiler_params=_params(1),
        name="attn_sample",
    )(q, ck, cv, kn, vn, za)


def _conv_body(u_ref, g_ref, st_ref, cw_ref, cb_ref, y_ref, ns_ref):
    u = u_ref[...]
    st = st_ref[0]
    t_len = u.shape[0]
    row = lax.broadcasted_iota(jnp.int32, u.shape, 0)
    u1 = jnp.where(row == 0, st[1:2, :], pltpu.roll(u, 1, axis=0))
    u2 = jnp.where(row == 0, st[0:1, :],
                   jnp.where(row == 1, st[1:2, :], pltpu.roll(u, 2, axis=0)))
    conv = cb_ref[...] + u2 * cw_ref[0:1, :] + u1 * cw_ref[1:2, :] + u * cw_ref[2:3, :]
    y_ref[...] = (g_ref[...].astype(F32) * conv).astype(y_ref.dtype)
    ns_ref[0] = u_ref[t_len - (CONV_WIDTH - 1):, :]


def _conv(u, gate, state, conv_w, conv_b, *, row_offset, t_len, tc):
    bsz = state.shape[0]
    ch = u.shape[1]
    tc = min(tc, ch)
    r = row_offset // t_len
    in_spec = pl.BlockSpec((t_len, tc), lambda b, c: (r + b, c))
    st_spec = pl.BlockSpec((1, CONV_WIDTH - 1, tc), lambda b, c: (b, 0, c))
    return pl.pallas_call(
        _conv_body,
        grid=(bsz, ch // tc),
        in_specs=[in_spec, in_spec, st_spec,
                  pl.BlockSpec((CONV_WIDTH, tc), lambda b, c: (0, c)),
                  pl.BlockSpec((1, tc), lambda b, c: (0, c))],
        out_specs=[pl.BlockSpec((t_len, tc), lambda b, c: (b, c)), st_spec],
        out_shape=[jax.ShapeDtypeStruct((bsz * t_len, ch), BF16),
                   jax.ShapeDtypeStruct((bsz, CONV_WIDTH - 1, ch), F32)],
        compiler_params=_params(2),
        name="short_conv",
    )(u, gate, state, conv_w, conv_b)


def _merge_body(ya_ref, yb_ref, wa_ref, wb_ref, ga_ref, gb_ref, o_ref):
    p_a = jnp.dot(ya_ref[...], wa_ref[...], preferred_element_type=F32)
    p_b = jnp.dot(yb_ref[...], wb_ref[...], preferred_element_type=F32)
    merged = ga_ref[...].astype(F32) * p_a + gb_ref[...].astype(F32) * p_b
    o_ref[...] = merged.astype(o_ref.dtype)


def _merge(y_a, y_b, w_a, w_b, gates, *, row_offset, tm, tn):
    m, ka = y_a.shape
    kb = y_b.shape[1]
    d = w_a.shape[1]
    tn = min(tn, d)
    nj = d // tn
    r = row_offset // tm
    return pl.pallas_call(
        _merge_body,
        grid=(m // tm, nj),
        in_specs=[pl.BlockSpec((tm, ka), lambda i, j: (i, 0)),
                  pl.BlockSpec((tm, kb), lambda i, j: (i, 0)),
                  pl.BlockSpec((ka, tn), lambda i, j: (0, j)),
                  pl.BlockSpec((kb, tn), lambda i, j: (0, j)),
                  pl.BlockSpec((tm, tn), lambda i, j: (r + i, j)),
                  pl.BlockSpec((tm, tn), lambda i, j: (r + i, nj + j))],
        out_specs=pl.BlockSpec((tm, tn), lambda i, j: (i, j)),
        out_shape=jax.ShapeDtypeStruct((m, d), BF16),
        compiler_params=_params(2),
        name="gated_merge",
    )(y_a, y_b, w_a, w_b, gates, gates)


def _out_body(m_ref, w_ref, x_ref, g_ref, b_ref, o_ref, *, alpha):
    pre = alpha * x_ref[...] + jnp.dot(m_ref[...], w_ref[...], preferred_element_type=F32)
    mu = jnp.mean(pre, axis=1, keepdims=True)
    cen = pre - mu
    var = jnp.mean(cen * cen, axis=1, keepdims=True)
    o_ref[...] = cen * lax.rsqrt(var + LN_EPS) * g_ref[...] + b_ref[...]


def _out_proj_ln(merged, w_o, x, ln_g, ln_b, *, alpha, tm):
    m, d = x.shape
    tm = min(tm, m)
    row_spec = lambda: pl.BlockSpec((tm, d), lambda i: (i, 0))
    const_spec = lambda r: pl.BlockSpec((r, d), lambda i: (0, 0), pipeline_mode=pl.Buffered(1))
    return pl.pallas_call(
        functools.partial(_out_body, alpha=alpha),
        grid=(m // tm,),
        in_specs=[row_spec(), const_spec(d), row_spec(), const_spec(1), const_spec(1)],
        out_specs=row_spec(),
        out_shape=jax.ShapeDtypeStruct((m, d), F32),
        compiler_params=_params(1, vmem_limit_bytes=OUT_PROJ_VMEM_LIMIT_BYTES),
        name="out_proj_ln",
    )(merged, w_o, x, ln_g, ln_b)


def _layer(x, past_k, past_v, past_conv, w_in, w_gates, b_in, conv_w, conv_b, w_a, w_b, w_o,
           ln_g, ln_b, rounded, *, n_heads, dh, alpha):
    bsz, t_len, d = x.shape
    m = bsz * t_len
    w_att = n_heads * dh
    w_conv = conv_w.shape[1]
    tm = min(ROW_TILE, m)
    x2d = x.reshape(m, d)
    q_scale = dh ** -0.5 * math.log2(math.e)

    o_q, o_k, o_v, o_za = 0, w_att, 2 * w_att, 3 * w_att
    o_bg = 4 * w_att
    o_cg, o_h, o_zb = o_bg + w_conv, o_bg + 2 * w_conv, o_bg + 3 * w_conv
    o_g = o_bg + 4 * w_conv
    ride = rounded is None
    xb = x2d.astype(BF16)
    gates_out = _proj(xb, w_gates, b_in[:, o_g:], [0], 2 * d, lambda a: (_sigmoid(a),), [BF16],
                      tm=tm, tn=1024, name="proj_gates",
                      round_cols=(w_in, o_g) if ride else None)
    gates = gates_out[-1]
    w_main = gates_out[0] if ride else rounded[0]

    proj = functools.partial(_proj, xb, w_main, b_in, tm=tm)
    (q,) = proj([o_q], w_att, lambda a: (a * q_scale,), [BF16], tn=1024, name="proj_q")
    k, kb16 = _proj_heads(xb, w_main, b_in, o_k, w_att, dh, tm=tm, tn=1024, name="proj_k")
    v, vb16 = _proj_heads(xb, w_main, b_in, o_v, w_att, dh, tm=tm, tn=1024, name="proj_v")
    za_out = proj([o_za], w_att, lambda a: (_silu(a),), [BF16], tn=1024, name="proj_za",
                  round_cols=(w_o, w_o.shape[1]) if ride else None)
    gate_b_out = proj([o_bg, o_zb], w_conv, lambda bg, zb: (_silu(zb) * bg,), [BF16], tn=512,
                      name="proj_gate_b", round_cols=(w_b, w_b.shape[1]) if ride else None)
    u_out = proj([o_cg, o_h], w_conv, lambda cg, h: (cg * h,), [F32], tn=512, name="proj_u",
                 round_cols=(w_a, w_a.shape[1]) if ride else None)
    za, gate_b, u = za_out[-1], gate_b_out[-1], u_out[-1]
    if ride:
        rounded = (w_main, u_out[0], gate_b_out[0], za_out[0])
    _, w_a, w_b, w_o = rounded

    if past_k is None:
        y_a = _attn_prompt(q, kb16, vb16, za, bsz=bsz, t_len=t_len, dh=dh, heads_per_step=4,
                           blk=ATT_BLOCK)
        past_conv = jnp.zeros((bsz, CONV_WIDTH - 1, w_conv), F32)
        tc = 512
    else:
        y_a = _attn_sample(q, past_k, past_v, kb16, vb16, za, row_offset=0, t_len=t_len,
                           group=4, chunk=ATT_BLOCK)
        tc = w_conv
    y_b, new_conv = _conv(u, gate_b, past_conv, conv_w, conv_b, row_offset=0, t_len=t_len, tc=tc)
    merged = _merge(y_a, y_b, w_a, w_b, gates, row_offset=0, tm=tm, tn=1024)
    y = _out_proj_ln(merged, w_o, x2d, ln_g, ln_b, alpha=alpha, tm=256)
    return (y.reshape(bsz, t_len, d), k.reshape(bsz, t_len, n_heads, dh),
            v.reshape(bsz, t_len, n_heads, dh), new_conv), rounded


def kernel(x_prompt, x_sample, cache_k, cache_v, state_conv, w_in, b_in, conv_w, conv_b,
           w_a, w_b, w_o, ln_g, ln_b):
    depth = w_in.shape[0]
    n_heads, dh = cache_k.shape[-2:]
    alpha = (2.0 * depth) ** 0.25
    xp, xs = x_prompt, x_sample
    kp, vp, cp, kn, vn, cn = [], [], [], [], [], []
    for l in range(depth):
        n_gate_cols = 2 * w_o.shape[-1]
        wts = (w_in[l], w_in[l][:, -n_gate_cols:].astype(BF16), b_in[l][None, :],
               conv_w[l], conv_b[l][None, :], w_a[l], w_b[l], w_o[l],
               ln_g[l][None, :], ln_b[l][None, :])
        layer = functools.partial(_layer, n_heads=n_heads, dh=dh, alpha=alpha)
        (xp, k1, v1, c1), rounded = layer(xp, None, None, None, *wts, None)
        (xs, k2, v2, c2), _ = layer(xs, cache_k[l], cache_v[l], state_conv[l], *wts, rounded)
        kp.append(k1); vp.append(v1); cp.append(c1)
        kn.append(k2); vn.append(v2); cn.append(c2)
    return (xp, xs, jnp.stack(kp), jnp.stack(vp), jnp.stack(cp),
            jnp.stack(kn), jnp.stack(vn), jnp.stack(cn))
```

```python
import functools
import math

import jax
import jax.numpy as jnp
from jax import lax
from jax.experimental import pallas as pl
from jax.experimental.pallas import tpu as pltpu

F32 = jnp.float32
BF16 = jnp.bfloat16

LN_EPS = 1e-5
CONV_WIDTH = 3
VMEM_LIMIT_BYTES = 58 * 1024 * 1024
OUT_PROJ_VMEM_LIMIT_BYTES = 60 * 1024 * 1024
LANES = 128
ATT_BLOCK = 256
ROW_TILE = 1024
MASKED_SCORE = -1e30
UNDERFLOW_LOG2 = -160.0
NT_DIMS = (((1,), (1,)), ((), ()))


def _params(n_grid_dims, vmem_limit_bytes=VMEM_LIMIT_BYTES):
    return pltpu.CompilerParams(
        dimension_semantics=("arbitrary",) * n_grid_dims,
        vmem_limit_bytes=vmem_limit_bytes,
    )


def _sigmoid(x):
    return 1.0 / (1.0 + jnp.exp(-x))


def _silu(x):
    return x * _sigmoid(x)


def _proj_body(*refs, n_slabs, n_out, epilogue):
    x_ref = refs[0]
    w_refs = refs[1:1 + n_slabs]
    b_refs = refs[1 + n_slabs:1 + 2 * n_slabs]
    o_refs = refs[-n_out:]
    x = x_ref[...]
    accs = [jnp.dot(x, w[...], preferred_element_type=F32) + b[...]
            for w, b in zip(w_refs, b_refs)]
    for o_ref, val in zip(o_refs, epilogue(*accs)):
        o_ref[...] = val.astype(o_ref.dtype)
    if len(refs) > 1 + 2 * n_slabs + n_out:
        refs[1 + 2 * n_slabs + 1][...] = refs[1 + 2 * n_slabs][...].astype(BF16)


def _proj(x, w, b, col_offsets, width, epilogue, out_dtypes, *, tm, tn, name,
          round_cols=None):
    m, k = x.shape
    n_slabs = len(col_offsets)
    tn = min(tn, width)
    ni, nj = m // tm, width // tn
    in_specs = [pl.BlockSpec((tm, k), lambda i, j: (i, 0))]
    for off in col_offsets:
        in_specs.append(pl.BlockSpec((k, tn), lambda i, j, o=off // tn: (0, o + j)))
    for off in col_offsets:
        in_specs.append(pl.BlockSpec((1, tn), lambda i, j, o=off // tn: (0, o + j)))
    out_specs = [pl.BlockSpec((tm, tn), lambda i, j: (i, j)) for _ in out_dtypes]
    out_shape = [jax.ShapeDtypeStruct((m, width), dt) for dt in out_dtypes]
    operands = [x, *([w] * n_slabs), *([b] * n_slabs)]
    if round_cols is not None:
        src, start, n_cols = round_cols
        chunk, rem = divmod(n_cols, ni * nj)
        assert rem == 0 and chunk % LANES == 0 and start % chunk == 0, (start, n_cols, ni, nj)
        first = start // chunk
        in_specs.append(pl.BlockSpec((src.shape[0], chunk),
                                     lambda i, j: (0, first + i * nj + j)))
        out_specs.insert(0, pl.BlockSpec((src.shape[0], chunk), lambda i, j: (0, i * nj + j)))
        out_shape.insert(0, jax.ShapeDtypeStruct((src.shape[0], n_cols), BF16))
        operands.append(src)
    return pl.pallas_call(
        functools.partial(_proj_body, n_slabs=n_slabs, n_out=len(out_dtypes), epilogue=epilogue),
        grid=(ni, nj),
        in_specs=in_specs,
        out_specs=out_specs,
        out_shape=out_shape,
        compiler_params=_params(2),
        name=name,
    )(*operands)


def _proj_heads_body(x_ref, w_ref, b_ref, of_ref, ob_ref):
    acc = jnp.dot(x_ref[...], w_ref[...], preferred_element_type=F32) + b_ref[...]
    ob_ref[...] = acc.astype(ob_ref.dtype)
    of_ref[...] = acc.reshape(of_ref.shape)


def _proj_heads(x, w, b, col_offset, width, dh, *, tm, tn, name):
    m, k = x.shape
    tn = min(tn, width)
    o = col_offset // tn
    return pl.pallas_call(
        _proj_heads_body,
        grid=(m // tm, width // tn),
        in_specs=[pl.BlockSpec((tm, k), lambda i, j: (i, 0)),
                  pl.BlockSpec((k, tn), lambda i, j: (0, o + j)),
                  pl.BlockSpec((1, tn), lambda i, j: (0, o + j))],
        out_specs=[pl.BlockSpec((tm, tn // dh, dh), lambda i, j: (i, j, 0)),
                   pl.BlockSpec((tm, tn), lambda i, j: (i, j))],
        out_shape=[jax.ShapeDtypeStruct((m, width // dh, dh), F32),
                   jax.ShapeDtypeStruct((m, width), BF16)],
        compiler_params=_params(2),
        name=name,
    )(x, w, b)


def _suffix_ones(n):
    j = lax.broadcasted_iota(jnp.int32, (n, n), 0)
    s = lax.broadcasted_iota(jnp.int32, (n, n), 1)
    return (j > s).astype(BF16)


def _causal_bias(n):
    row = lax.broadcasted_iota(jnp.int32, (n, n), 0)
    col = lax.broadcasted_iota(jnp.int32, (n, n), 1)
    return jnp.where(col < row, 0.0, MASKED_SCORE).astype(F32)


def _sb_logs(z):
    ls = jnp.minimum(z, 0.0) - jnp.log2(1.0 + jnp.exp2(-jnp.abs(z)))
    return ls, ls - z


def _add_lane_tiled(a, r):
    return jnp.concatenate(
        [a[:, c:c + LANES] + r for c in range(0, a.shape[1], LANES)], axis=1)


def _sb_suffix(lk, u_mat):
    return jnp.dot(lk.astype(BF16), u_mat, preferred_element_type=F32)


def _attn_prompt_body(q_ref, k_ref, v_ref, za_ref, o_ref,
                      bias_ref, z_buf, tw_buf, af_buf, run_ref, acc_ref, *, heads, dh, blk):
    t_len = q_ref.shape[0]
    nq = t_len // blk
    u_mat = _suffix_ones(blk)
    bias_ref[0:blk, :] = _causal_bias(blk)
    bias_ref[blk:2 * blk, :] = jnp.zeros((blk, blk), F32)
    hsl = [slice(h * dh, (h + 1) * dh) for h in range(heads)]
    for h in range(heads):
        z_buf[1, h] = jnp.full((blk, blk), MASKED_SCORE, F32)
        tw_buf[1, h] = jnp.full((blk, blk), MASKED_SCORE, F32)
        af_buf[1, h] = jnp.zeros((blk, blk), F32)
        run_ref[h] = jnp.zeros((blk, LANES), F32)
        acc_ref[h] = jnp.zeros((blk, dh), F32)

    def step(st, new):
        (qa, ka), (qb, kb, last_b, valid_b), (qc, kc, last_c, valid_c) = st
        old = 1 - new
        first = kb == qb
        af_new = []
        run_max = None
        for h in range(heads):
            ls, lk = _sb_logs(z_buf[old, h])
            off = jnp.where(first, 0.0, run_ref[h])
            off = jnp.where(valid_b == 1, off, MASKED_SCORE)
            tw_buf[new, h] = _add_lane_tiled(ls, off)
            af_new.append(_sb_suffix(lk, u_mat))
            run = off + jnp.broadcast_to(jnp.sum(lk, axis=1, keepdims=True), off.shape)
            run_ref[h] = run
            run_max = jnp.max(run) if run_max is None else jnp.maximum(run_max, jnp.max(run))
        dead = jnp.logical_and(valid_b == 1, run_max < UNDERFLOW_LOG2)
        cut = jnp.logical_and(dead, jnp.logical_and(qa == qb, qa < nq))
        qs = jnp.where(cut, qa + 1, qa)
        ks = jnp.where(cut, qa + 1, ka)
        last_b = jnp.maximum(last_b, cut.astype(jnp.int32))
        valid_s = qs < nq
        q0 = pl.multiple_of(jnp.minimum(qs, nq - 1) * blk, blk)
        k0 = pl.multiple_of(jnp.minimum(ks, nq - 1) * blk, blk)
        b0 = pl.multiple_of(jnp.where(ks == qs, 0, blk), blk)
        bias = bias_ref[pl.ds(b0, blk), :]
        z_new = [lax.dot_general(q_ref[pl.ds(q0, blk), hsl[h]],
                                 k_ref[pl.ds(k0, blk), hsl[h]], NT_DIMS,
                                 preferred_element_type=F32) for h in range(heads)]
        vc0 = pl.multiple_of(kc * blk, blk)
        qc0 = pl.multiple_of(qc * blk, blk)
        restart = jnp.logical_and(last_c == 1, qc < nq - 1)
        for h in range(heads):
            w = jnp.exp2(tw_buf[old, h] + af_buf[old, h]).astype(BF16)
            acc = acc_ref[h] + jnp.dot(w, v_ref[pl.ds(vc0, blk), hsl[h]],
                                       preferred_element_type=F32)
            gate = za_ref[pl.ds(qc0, blk), hsl[h]].astype(F32)
            o_ref[pl.ds(qc0, blk), hsl[h]] = (acc * gate).astype(o_ref.dtype)
            acc_ref[h] = jnp.where(restart, 0.0, acc)
        for h in range(heads):
            z_buf[new, h] = z_new[h] + bias
            af_buf[new, h] = af_new[h]
        ends = ks == 0
        nxt_q = jnp.where(jnp.logical_and(ends, valid_s), qs + 1, qs)
        nxt_k = jnp.where(ends, nxt_q, ks - 1)
        return ((nxt_q, nxt_k),
                (jnp.minimum(qs, nq - 1), jnp.minimum(ks, nq - 1), ends.astype(jnp.int32),
                 valid_s.astype(jnp.int32)),
                (qb, kb, last_b, valid_b))

    def in_flight(st):
        (qa, _), (_, _, _, valid_b), (_, _, _, valid_c) = st
        return jnp.logical_or(qa < nq, jnp.logical_or(valid_b == 1, valid_c == 1))

    zero = jnp.int32(0)
    lax.while_loop(in_flight, lambda st: step(step(st, 0), 1),
                   ((zero, zero), (zero, zero, zero, zero), (zero, zero, zero, zero)))


def _attn_prompt(q, k, v, za, *, bsz, t_len, dh, heads_per_step, blk):
    width = q.shape[1]
    hw = heads_per_step * dh
    spec = pl.BlockSpec((t_len, hw), lambda b, g: (b, g))
    return pl.pallas_call(
        functools.partial(_attn_prompt_body, heads=heads_per_step, dh=dh, blk=blk),
        grid=(bsz, width // hw),
        in_specs=[spec, spec, spec, spec],
        out_specs=spec,
        out_shape=jax.ShapeDtypeStruct((bsz * t_len, width), BF16),
        scratch_shapes=[pltpu.VMEM((2 * blk, blk), F32),
                        pltpu.VMEM((2, heads_per_step, blk, blk), F32),
                        pltpu.VMEM((2, heads_per_step, blk, blk), F32),
                        pltpu.VMEM((2, heads_per_step, blk, blk), F32),
                        pltpu.VMEM((heads_per_step, blk, LANES), F32),
                        pltpu.VMEM((heads_per_step, blk, dh), F32)],
        compiler_params=_params(2),
        name="attn_prompt",
    )(q, k, v, za)


def _attn_sample_body(q_ref, ck_hbm, cv_hbm, kn_ref, vn_ref, za_ref, o_ref, k_buf, v_buf, sems,
                      *, group, chunk):
    b = pl.program_id(0)
    n_streams = pl.num_programs(0)
    t_len = q_ref.shape[0]
    _, n_heads, p_len, dh = k_buf.shape
    n_chunks = p_len // chunk
    u_mat = _suffix_ones(chunk)
    u_new = _suffix_ones(t_len)
    bias_new = _causal_bias(t_len)
    slot = lax.rem(b, 2)

    def head_copies(stream, s):
        return [pltpu.make_async_copy(src.at[stream, :, h, :], dst.at[s, h], sems.at[s, a, h])
                for a, (src, dst) in enumerate(((ck_hbm, k_buf), (cv_hbm, v_buf)))
                for h in range(n_heads)]

    @pl.when(b == 0)
    def _():
        for cp in head_copies(0, 0):
            cp.start()

    @pl.when(b + 1 < n_streams)
    def _():
        for cp in head_copies(b + 1, 1 - slot):
            cp.start()

    for cp in head_copies(b, slot):
        cp.wait()

    def past_rows(buf, head):
        return buf[slot, head].astype(BF16)

    ck_ref, cv_ref = k_buf, v_buf
    for g in range(n_heads // group):
        heads = range(g * group, (g + 1) * group)
        hsl = [slice(h * dh, (h + 1) * dh) for h in heads]
        z_past, z_new = [], []
        for h, hs in zip(heads, hsl):
            q = q_ref[:, hs]
            z_past.append(lax.dot_general(q, past_rows(ck_ref, h), NT_DIMS,
                                          preferred_element_type=F32))
            z_new.append(lax.dot_general(q, kn_ref[:, hs], NT_DIMS,
                                         preferred_element_type=F32) + bias_new)
        z_past = jnp.concatenate(z_past, axis=0)
        z_new = jnp.concatenate(z_new, axis=0)

        ls_new, lk_new = _sb_logs(z_new)
        w_new = jnp.exp2(ls_new + _sb_suffix(lk_new, u_new)).astype(BF16)
        off = jnp.sum(lk_new, axis=1, keepdims=True)
        ls_past, lk_past = _sb_logs(z_past)
        w_chunks = [None] * n_chunks
        for c in range(n_chunks - 1, -1, -1):
            cs = slice(c * chunk, (c + 1) * chunk)
            lk = lk_past[:, cs]
            w_chunks[c] = jnp.exp2(ls_past[:, cs] + _sb_suffix(lk, u_mat) + off).astype(BF16)
            off = off + jnp.sum(lk, axis=1, keepdims=True)
        w_past = jnp.concatenate(w_chunks, axis=1)

        for n, (h, hs) in enumerate(zip(heads, hsl)):
            rows = slice(n * t_len, (n + 1) * t_len)
            acc = (jnp.dot(w_past[rows], past_rows(cv_ref, h), preferred_element_type=F32)
                   + jnp.dot(w_new[rows], vn_ref[:, hs], preferred_element_type=F32))
            o_ref[:, hs] = (acc * za_ref[:, hs].astype(F32)).astype(o_ref.dtype)


def _attn_sample(q, ck, cv, kn, vn, za, *, row_offset, t_len, group, chunk):
    bsz, p_len, n_heads, dh = ck.shape
    width = q.shape[1]
    r = row_offset // t_len
    shifted = pl.BlockSpec((t_len, width), lambda b: (r + b, 0))
    local = pl.BlockSpec((t_len, width), lambda b: (b, 0))
    in_hbm = pl.BlockSpec(memory_space=pltpu.HBM)
    return pl.pallas_call(
        functools.partial(_attn_sample_body, group=group, chunk=chunk),
        grid=(bsz,),
        in_specs=[shifted, in_hbm, in_hbm, local, local, shifted],
        out_specs=local,
        out_shape=jax.ShapeDtypeStruct((bsz * t_len, width), BF16),
        scratch_shapes=[pltpu.VMEM((2, n_heads, p_len, dh), F32),
                        pltpu.VMEM((2, n_heads, p_len, dh), F32),
                        pltpu.SemaphoreType.DMA((2, 2, n_heads))],
        compiler_params=_params(1),
        name="attn_sample",
    )(q, ck, cv, kn, vn, za)


def _conv_body(u_ref, g_ref, st_ref, cw_ref, cb_ref, y_ref, ns_ref):
    u = u_ref[...]
    st = st_ref[0]
    t_len = u.shape[0]
    row = lax.broadcasted_iota(jnp.int32, u.shape, 0)
    u1 = jnp.where(row == 0, st[1:2, :], pltpu.roll(u, 1, axis=0))
    u2 = jnp.where(row == 0, st[0:1, :],
                   jnp.where(row == 1, st[1:2, :], pltpu.roll(u, 2, axis=0)))
    conv = cb_ref[...] + u2 * cw_ref[0:1, :] + u1 * cw_ref[1:2, :] + u * cw_ref[2:3, :]
    y_ref[...] = (g_ref[...].astype(F32) * conv).astype(y_ref.dtype)
    ns_ref[0] = u_ref[t_len - (CONV_WIDTH - 1):, :]


def _conv(u, gate, state, conv_w, conv_b, *, row_offset, t_len, tc):
    bsz = state.shape[0]
    ch = u.shape[1]
    tc = min(tc, ch)
    r = row_offset // t_len
    in_spec = pl.BlockSpec((t_len, tc), lambda b, c: (r + b, c))
    st_spec = pl.BlockSpec((1, CONV_WIDTH - 1, tc), lambda b, c: (b, 0, c))
    return pl.pallas_call(
        _conv_body,
        grid=(bsz, ch // tc),
        in_specs=[in_spec, in_spec, st_spec,
                  pl.BlockSpec((CONV_WIDTH, tc), lambda b, c: (0, c)),
                  pl.BlockSpec((1, tc), lambda b, c: (0, c))],
        out_specs=[pl.BlockSpec((t_len, tc), lambda b, c: (b, c)), st_spec],
        out_shape=[jax.ShapeDtypeStruct((bsz * t_len, ch), BF16),
                   jax.ShapeDtypeStruct((bsz, CONV_WIDTH - 1, ch), F32)],
        compiler_params=_params(2),
        name="short_conv",
    )(u, gate, state, conv_w, conv_b)


def _merge_body(ya_ref, yb_ref, wa_ref, wb_ref, ga_ref, gb_ref, o_ref):
    p_a = jnp.dot(ya_ref[...], wa_ref[...], preferred_element_type=F32)
    p_b = jnp.dot(yb_ref[...], wb_ref[...], preferred_element_type=F32)
    merged = ga_ref[...].astype(F32) * p_a + gb_ref[...].astype(F32) * p_b
    o_ref[...] = merged.astype(o_ref.dtype)


def _merge(y_a, y_b, w_a, w_b, gates, *, row_offset, tm, tn):
    m, ka = y_a.shape
    kb = y_b.shape[1]
    d = w_a.shape[1]
    tn = min(tn, d)
    nj = d // tn
    r = row_offset // tm
    return pl.pallas_call(
        _merge_body,
        grid=(m // tm, nj),
        in_specs=[pl.BlockSpec((tm, ka), lambda i, j: (i, 0)),
                  pl.BlockSpec((tm, kb), lambda i, j: (i, 0)),
                  pl.BlockSpec((ka, tn), lambda i, j: (0, j)),
                  pl.BlockSpec((kb, tn), lambda i, j: (0, j)),
                  pl.BlockSpec((tm, tn), lambda i, j: (r + i, j)),
                  pl.BlockSpec((tm, tn), lambda i, j: (r + i, nj + j))],
        out_specs=pl.BlockSpec((tm, tn), lambda i, j: (i, j)),
        out_shape=jax.ShapeDtypeStruct((m, d), BF16),
        compiler_params=_params(2),
        name="gated_merge",
    )(y_a, y_b, w_a, w_b, gates, gates)


def _out_body(m_ref, w_ref, x_ref, g_ref, b_ref, o_ref, *, alpha):
    pre = alpha * x_ref[...] + jnp.dot(m_ref[...], w_ref[...], preferred_element_type=F32)
    mu = jnp.mean(pre, axis=1, keepdims=True)
    cen = pre - mu
    var = jnp.mean(cen * cen, axis=1, keepdims=True)
    o_ref[...] = cen * lax.rsqrt(var + LN_EPS) * g_ref[...] + b_ref[...]


def _out_proj_ln(merged, w_o, x, ln_g, ln_b, *, alpha, tm):
    m, d = x.shape
    tm = min(tm, m)
    row_spec = lambda: pl.BlockSpec((tm, d), lambda i: (i, 0))
    const_spec = lambda r: pl.BlockSpec((r, d), lambda i: (0, 0), pipeline_mode=pl.Buffered(1))
    return pl.pallas_call(
        functools.partial(_out_body, alpha=alpha),
        grid=(m // tm,),
        in_specs=[row_spec(), const_spec(d), row_spec(), const_spec(1), const_spec(1)],
        out_specs=row_spec(),
        out_shape=jax.ShapeDtypeStruct((m, d), F32),
        compiler_params=_params(1, vmem_limit_bytes=OUT_PROJ_VMEM_LIMIT_BYTES),
        name="out_proj_ln",
    )(merged, w_o, x, ln_g, ln_b)


def _layer(x, past_k, past_v, past_conv, w_in, b_in, conv_w, conv_b, w_a, w_b, w_o,
           ln_g, ln_b, rounded, *, n_heads, dh, alpha):
    bsz, t_len, d = x.shape
    m = bsz * t_len
    w_att = n_heads * dh
    w_conv = conv_w.shape[1]
    tm = min(ROW_TILE, m)
    x2d = x.reshape(m, d)
    q_scale = dh ** -0.5 * math.log2(math.e)

    o_q, o_k, o_v, o_za = 0, w_att, 2 * w_att, 3 * w_att
    o_bg = 4 * w_att
    o_cg, o_h, o_zb = o_bg + w_conv, o_bg + 2 * w_conv, o_bg + 3 * w_conv
    o_g = o_bg + 4 * w_conv
    ride = rounded is None
    xb = x2d.astype(BF16)
    conv_in = lambda cg, h: (cg * h,)
    if ride:
        cols_u = slice(o_cg, o_cg + 2 * w_conv)
        w_gates, u = _proj(xb, w_in[:, cols_u].astype(BF16), b_in[:, cols_u], [0, w_conv],
                           w_conv, conv_in, [F32], tm=tm, tn=512, name="proj_u",
                           round_cols=(w_in, o_g, 2 * d))
        w_main, gates = _proj(xb, w_gates, b_in[:, o_g:], [0], 2 * d, lambda a: (_sigmoid(a),),
                              [BF16], tm=tm, tn=1024, name="proj_gates",
                              round_cols=(w_in, 0, o_g))
    else:
        w_main, w_gates = rounded[0], rounded[4]
        (gates,) = _proj(xb, w_gates, b_in[:, o_g:], [0], 2 * d, lambda a: (_sigmoid(a),),
                         [BF16], tm=tm, tn=1024, name="proj_gates")

    proj = functools.partial(_proj, xb, w_main, b_in, tm=tm)
    if not ride:
        (u,) = proj([o_cg, o_h], w_conv, conv_in, [F32], tn=512, name="proj_u")
    q_out = proj([o_q], w_att, lambda a: (a * q_scale,), [BF16], tn=1024, name="proj_q",
                 round_cols=(w_a, 0, w_a.shape[1]) if ride else None)
    k, kb16 = _proj_heads(xb, w_main, b_in, o_k, w_att, dh, tm=tm, tn=1024, name="proj_k")
    v, vb16 = _proj_heads(xb, w_main, b_in, o_v, w_att, dh, tm=tm, tn=1024, name="proj_v")
    za_out = proj([o_za], w_att, lambda a: (_silu(a),), [BF16], tn=1024, name="proj_za",
                  round_cols=(w_o, 0, w_o.shape[1]) if ride else None)
    gate_b_out = proj([o_bg, o_zb], w_conv, lambda bg, zb: (_silu(zb) * bg,), [BF16], tn=512,
                      name="proj_gate_b", round_cols=(w_b, 0, w_b.shape[1]) if ride else None)
    q, za, gate_b = q_out[-1], za_out[-1], gate_b_out[-1]
    if ride:
        rounded = (w_main, q_out[0], gate_b_out[0], za_out[0], w_gates)
    _, w_a, w_b, w_o, _ = rounded

    if past_k is None:
        y_a = _attn_prompt(q, kb16, vb16, za, bsz=bsz, t_len=t_len, dh=dh, heads_per_step=4,
                           blk=ATT_BLOCK)
        past_conv = jnp.zeros((bsz, CONV_WIDTH - 1, w_conv), F32)
        tc = 512
    else:
        y_a = _attn_sample(q, past_k, past_v, kb16, vb16, za, row_offset=0, t_len=t_len,
                           group=4, chunk=ATT_BLOCK)
        tc = w_conv
    y_b, new_conv = _conv(u, gate_b, past_conv, conv_w, conv_b, row_offset=0, t_len=t_len, tc=tc)
    merged = _merge(y_a, y_b, w_a, w_b, gates, row_offset=0, tm=tm, tn=1024)
    y = _out_proj_ln(merged, w_o, x2d, ln_g, ln_b, alpha=alpha, tm=256)
    return (y.reshape(bsz, t_len, d), k.reshape(bsz, t_len, n_heads, dh),
            v.reshape(bsz, t_len, n_heads, dh), new_conv), rounded


def kernel(x_prompt, x_sample, cache_k, cache_v, state_conv, w_in, b_in, conv_w, conv_b,
           w_a, w_b, w_o, ln_g, ln_b):
    depth = w_in.shape[0]
    n_heads, dh = cache_k.shape[-2:]
    alpha = (2.0 * depth) ** 0.25
    xp, xs = x_prompt, x_sample
    kp, vp, cp, kn, vn, cn = [], [], [], [], [], []
    for l in range(depth):
        wts = (w_in[l], b_in[l][None, :], conv_w[l], conv_b[l][None, :], w_a[l], w_b[l], w_o[l],
               ln_g[l][None, :], ln_b[l][None, :])
        layer = functools.partial(_layer, n_heads=n_heads, dh=dh, alpha=alpha)
        (xp, k1, v1, c1), rounded = layer(xp, None, None, None, *wts, None)
        (xs, k2, v2, c2), _ = layer(xs, cache_k[l], cache_v[l], state_conv[l], *wts, rounded)
        kp.append(k1); vp.append(v1); cp.append(c1)
        kn.append(k2); vn.append(v2); cn.append(c2)
    return (xp, xs, jnp.stack(kp), jnp.stack(vp), jnp.stack(cp),
            jnp.stack(kn), jnp.stack(vn), jnp.stack(cn))
```

```python
import functools
import math

import jax
import jax.numpy as jnp
from jax import lax
from jax.experimental import pallas as pl
from jax.experimental.pallas import tpu as pltpu

F32 = jnp.float32
BF16 = jnp.bfloat16

LN_EPS = 1e-5
CONV_WIDTH = 3
VMEM_LIMIT_BYTES = 58 * 1024 * 1024
OUT_PROJ_VMEM_LIMIT_BYTES = 60 * 1024 * 1024
LANES = 128
ATT_BLOCK = 256
ROW_TILE = 1024
MASKED_SCORE = -1e30
UNDERFLOW_LOG2 = -160.0
NT_DIMS = (((1,), (1,)), ((), ()))


def _params(n_grid_dims, vmem_limit_bytes=VMEM_LIMIT_BYTES):
    return pltpu.CompilerParams(
        dimension_semantics=("arbitrary",) * n_grid_dims,
        vmem_limit_bytes=vmem_limit_bytes,
    )


def _sigmoid(x):
    return 1.0 / (1.0 + jnp.exp(-x))


def _silu(x):
    return x * _sigmoid(x)


def _proj_body(*refs, n_slabs, n_out, epilogue):
    x_ref = refs[0]
    w_refs = refs[1:1 + n_slabs]
    b_refs = refs[1 + n_slabs:1 + 2 * n_slabs]
    o_refs = refs[-n_out:]
    x = x_ref[...]
    accs = [jnp.dot(x, w[...], preferred_element_type=F32) + b[...]
            for w, b in zip(w_refs, b_refs)]
    for o_ref, val in zip(o_refs, epilogue(*accs)):
        o_ref[...] = val.astype(o_ref.dtype)
    if len(refs) > 1 + 2 * n_slabs + n_out:
        refs[1 + 2 * n_slabs + 1][...] = refs[1 + 2 * n_slabs][...].astype(BF16)


def _proj(x, w, b, col_offsets, width, epilogue, out_dtypes, *, tm, tn, name,
          round_cols=None):
    m, k = x.shape
    n_slabs = len(col_offsets)
    tn = min(tn, width)
    ni, nj = m // tm, width // tn
    in_specs = [pl.BlockSpec((tm, k), lambda i, j: (i, 0))]
    for off in col_offsets:
        in_specs.append(pl.BlockSpec((k, tn), lambda i, j, o=off // tn: (0, o + j)))
    for off in col_offsets:
        in_specs.append(pl.BlockSpec((1, tn), lambda i, j, o=off // tn: (0, o + j)))
    out_specs = [pl.BlockSpec((tm, tn), lambda i, j: (i, j)) for _ in out_dtypes]
    out_shape = [jax.ShapeDtypeStruct((m, width), dt) for dt in out_dtypes]
    operands = [x, *([w] * n_slabs), *([b] * n_slabs)]
    if round_cols is not None:
        src, start, n_cols = round_cols
        chunk, rem = divmod(n_cols, ni * nj)
        assert rem == 0 and chunk % LANES == 0 and start % chunk == 0, (start, n_cols, ni, nj)
        first = start // chunk
        in_specs.append(pl.BlockSpec((src.shape[0], chunk),
                                     lambda i, j: (0, first + i * nj + j)))
        out_specs.insert(0, pl.BlockSpec((src.shape[0], chunk), lambda i, j: (0, i * nj + j)))
        out_shape.insert(0, jax.ShapeDtypeStruct((src.shape[0], n_cols), BF16))
        operands.append(src)
    return pl.pallas_call(
        functools.partial(_proj_body, n_slabs=n_slabs, n_out=len(out_dtypes), epilogue=epilogue),
        grid=(ni, nj),
        in_specs=in_specs,
        out_specs=out_specs,
        out_shape=out_shape,
        compiler_params=_params(2),
        name=name,
    )(*operands)


def _proj_heads_body(x_ref, w_ref, b_ref, of_ref, ob_ref):
    acc = jnp.dot(x_ref[...], w_ref[...], preferred_element_type=F32) + b_ref[...]
    ob_ref[...] = acc.astype(ob_ref.dtype)
    of_ref[...] = acc.reshape(of_ref.shape)


def _proj_heads(x, w, b, col_offset, width, dh, *, tm, tn, name):
    m, k = x.shape
    tn = min(tn, width)
    o = col_offset // tn
    return pl.pallas_call(
        _proj_heads_body,
        grid=(m // tm, width // tn),
        in_specs=[pl.BlockSpec((tm, k), lambda i, j: (i, 0)),
                  pl.BlockSpec((k, tn), lambda i, j: (0, o + j)),
                  pl.BlockSpec((1, tn), lambda i, j: (0, o + j))],
        out_specs=[pl.BlockSpec((tm, tn // dh, dh), lambda i, j: (i, j, 0)),
                   pl.BlockSpec((tm, tn), lambda i, j: (i, j))],
        out_shape=[jax.ShapeDtypeStruct((m, width // dh, dh), F32),
                   jax.ShapeDtypeStruct((m, width), BF16)],
        compiler_params=_params(2),
        name=name,
    )(x, w, b)


def _suffix_ones(n):
    j = lax.broadcasted_iota(jnp.int32, (n, n), 0)
    s = lax.broadcasted_iota(jnp.int32, (n, n), 1)
    return (j > s).astype(BF16)


def _causal_bias(n):
    row = lax.broadcasted_iota(jnp.int32, (n, n), 0)
    col = lax.broadcasted_iota(jnp.int32, (n, n), 1)
    return jnp.where(col < row, 0.0, MASKED_SCORE).astype(F32)


def _sb_logs(z):
    ls = jnp.minimum(z, 0.0) - jnp.log2(1.0 + jnp.exp2(-jnp.abs(z)))
    return ls, ls - z


def _add_lane_tiled(a, r):
    return jnp.concatenate(
        [a[:, c:c + LANES] + r for c in range(0, a.shape[1], LANES)], axis=1)


def _sb_suffix(lk, u_mat):
    return jnp.dot(lk.astype(BF16), u_mat, preferred_element_type=F32)


def _attn_prompt_body(q_ref, k_ref, v_ref, za_ref, o_ref,
                      bias_ref, z_buf, tw_buf, af_buf, run_ref, acc_ref, *, heads, dh, blk):
    t_len = q_ref.shape[0]
    nq = t_len // blk
    u_mat = _suffix_ones(blk)
    bias_ref[0:blk, :] = _causal_bias(blk)
    bias_ref[blk:2 * blk, :] = jnp.zeros((blk, blk), F32)
    hsl = [slice(h * dh, (h + 1) * dh) for h in range(heads)]
    for h in range(heads):
        z_buf[1, h] = jnp.full((blk, blk), MASKED_SCORE, F32)
        tw_buf[1, h] = jnp.full((blk, blk), MASKED_SCORE, F32)
        af_buf[1, h] = jnp.zeros((blk, blk), F32)
        run_ref[h] = jnp.zeros((blk, LANES), F32)
        acc_ref[h] = jnp.zeros((blk, dh), F32)

    def step(st, new):
        (qa, ka), (qb, kb, last_b, valid_b), (qc, kc, last_c, valid_c) = st
        old = 1 - new
        first = kb == qb
        af_new = []
        run_max = None
        for h in range(heads):
            ls, lk = _sb_logs(z_buf[old, h])
            off = jnp.where(first, 0.0, run_ref[h])
            off = jnp.where(valid_b == 1, off, MASKED_SCORE)
            tw_buf[new, h] = _add_lane_tiled(ls, off)
            af_new.append(_sb_suffix(lk, u_mat))
            run = off + jnp.broadcast_to(jnp.sum(lk, axis=1, keepdims=True), off.shape)
            run_ref[h] = run
            run_max = jnp.max(run) if run_max is None else jnp.maximum(run_max, jnp.max(run))
        dead = jnp.logical_and(valid_b == 1, run_max < UNDERFLOW_LOG2)
        cut = jnp.logical_and(dead, jnp.logical_and(qa == qb, qa < nq))
        qs = jnp.where(cut, qa + 1, qa)
        ks = jnp.where(cut, qa + 1, ka)
        last_b = jnp.maximum(last_b, cut.astype(jnp.int32))
        valid_s = qs < nq
        q0 = pl.multiple_of(jnp.minimum(qs, nq - 1) * blk, blk)
        k0 = pl.multiple_of(jnp.minimum(ks, nq - 1) * blk, blk)
        b0 = pl.multiple_of(jnp.where(ks == qs, 0, blk), blk)
        bias = bias_ref[pl.ds(b0, blk), :]
        z_new = [lax.dot_general(q_ref[pl.ds(q0, blk), hsl[h]],
                                 k_ref[pl.ds(k0, blk), hsl[h]], NT_DIMS,
                                 preferred_element_type=F32) for h in range(heads)]
        vc0 = pl.multiple_of(kc * blk, blk)
        qc0 = pl.multiple_of(qc * blk, blk)
        restart = jnp.logical_and(last_c == 1, qc < nq - 1)
        for h in range(heads):
            w = jnp.exp2(tw_buf[old, h] + af_buf[old, h]).astype(BF16)
            acc = acc_ref[h] + jnp.dot(w, v_ref[pl.ds(vc0, blk), hsl[h]],
                                       preferred_element_type=F32)
            gate = za_ref[pl.ds(qc0, blk), hsl[h]].astype(F32)
            o_ref[pl.ds(qc0, blk), hsl[h]] = (acc * gate).astype(o_ref.dtype)
            acc_ref[h] = jnp.where(restart, 0.0, acc)
        for h in range(heads):
            z_buf[new, h] = z_new[h] + bias
            af_buf[new, h] = af_new[h]
        ends = ks == 0
        nxt_q = jnp.where(jnp.logical_and(ends, valid_s), qs + 1, qs)
        nxt_k = jnp.where(ends, nxt_q, ks - 1)
        return ((nxt_q, nxt_k),
                (jnp.minimum(qs, nq - 1), jnp.minimum(ks, nq - 1), ends.astype(jnp.int32),
                 valid_s.astype(jnp.int32)),
                (qb, kb, last_b, valid_b))

    def in_flight(st):
        (qa, _), (_, _, _, valid_b), (_, _, _, valid_c) = st
        return jnp.logical_or(qa < nq, jnp.logical_or(valid_b == 1, valid_c == 1))

    zero = jnp.int32(0)
    lax.while_loop(in_flight, lambda st: step(step(st, 0), 1),
                   ((zero, zero), (zero, zero, zero, zero), (zero, zero, zero, zero)))


def _attn_prompt(q, k, v, za, *, bsz, t_len, dh, heads_per_step, blk):
    width = q.shape[1]
    hw = heads_per_step * dh
    spec = pl.BlockSpec((t_len, hw), lambda b, g: (b, g))
    return pl.pallas_call(
        functools.partial(_attn_prompt_body, heads=heads_per_step, dh=dh, blk=blk),
        grid=(bsz, width // hw),
        in_specs=[spec, spec, spec, spec],
        out_specs=spec,
        out_shape=jax.ShapeDtypeStruct((bsz * t_len, width), BF16),
        scratch_shapes=[pltpu.VMEM((2 * blk, blk), F32),
                        pltpu.VMEM((2, heads_per_step, blk, blk), F32),
                        pltpu.VMEM((2, heads_per_step, blk, blk), F32),
                        pltpu.VMEM((2, heads_per_step, blk, blk), F32),
                        pltpu.VMEM((heads_per_step, blk, LANES), F32),
                        pltpu.VMEM((heads_per_step, blk, dh), F32)],
        compiler_params=_params(2),
        name="attn_prompt",
    )(q, k, v, za)


def _attn_sample_body(q_ref, ck_hbm, cv_hbm, kn_ref, vn_ref, za_ref, o_ref, k_buf, v_buf, sems,
                      *, group, chunk):
    b = pl.program_id(0)
    n_streams = pl.num_programs(0)
    t_len = q_ref.shape[0]
    _, n_heads, p_len, dh = k_buf.shape
    n_chunks = p_len // chunk
    u_mat = _suffix_ones(chunk)
    u_new = _suffix_ones(t_len)
    bias_new = _causal_bias(t_len)
    slot = lax.rem(b, 2)

    def head_copies(stream, s):
        return [pltpu.make_async_copy(src.at[stream, :, h, :], dst.at[s, h], sems.at[s, a, h])
                for a, (src, dst) in enumerate(((ck_hbm, k_buf), (cv_hbm, v_buf)))
                for h in range(n_heads)]

    @pl.when(b == 0)
    def _():
        for n, cp in enumerate(head_copies(0, 0)):
            cp.start(priority=n % 2)

    @pl.when(b + 1 < n_streams)
    def _():
        for n, cp in enumerate(head_copies(b + 1, 1 - slot)):
            cp.start(priority=n % 2)

    for cp in head_copies(b, slot):
        cp.wait()

    def past_rows(buf, head):
        return buf[slot, head].astype(BF16)

    ck_ref, cv_ref = k_buf, v_buf
    for g in range(n_heads // group):
        heads = range(g * group, (g + 1) * group)
        hsl = [slice(h * dh, (h + 1) * dh) for h in heads]
        z_past, z_new = [], []
        for h, hs in zip(heads, hsl):
            q = q_ref[:, hs]
            z_past.append(lax.dot_general(q, past_rows(ck_ref, h), NT_DIMS,
                                          preferred_element_type=F32))
            z_new.append(lax.dot_general(q, kn_ref[:, hs], NT_DIMS,
                                         preferred_element_type=F32) + bias_new)
        z_past = jnp.concatenate(z_past, axis=0)
        z_new = jnp.concatenate(z_new, axis=0)

        ls_new, lk_new = _sb_logs(z_new)
        w_new = jnp.exp2(ls_new + _sb_suffix(lk_new, u_new)).astype(BF16)
        off = jnp.sum(lk_new, axis=1, keepdims=True)
        ls_past, lk_past = _sb_logs(z_past)
        w_chunks = [None] * n_chunks
        for c in range(n_chunks - 1, -1, -1):
            cs = slice(c * chunk, (c + 1) * chunk)
            lk = lk_past[:, cs]
            w_chunks[c] = jnp.exp2(ls_past[:, cs] + _sb_suffix(lk, u_mat) + off).astype(BF16)
            off = off + jnp.sum(lk, axis=1, keepdims=True)
        w_past = jnp.concatenate(w_chunks, axis=1)

        for n, (h, hs) in enumerate(zip(heads, hsl)):
            rows = slice(n * t_len, (n + 1) * t_len)
            acc = (jnp.dot(w_past[rows], past_rows(cv_ref, h), preferred_element_type=F32)
                   + jnp.dot(w_new[rows], vn_ref[:, hs], preferred_element_type=F32))
            o_ref[:, hs] = (acc * za_ref[:, hs].astype(F32)).astype(o_ref.dtype)


def _attn_sample(q, ck, cv, kn, vn, za, *, row_offset, t_len, group, chunk):
    bsz, p_len, n_heads, dh = ck.shape
    width = q.shape[1]
    r = row_offset // t_len
    shifted = pl.BlockSpec((t_len, width), lambda b: (r + b, 0))
    local = pl.BlockSpec((t_len, width), lambda b: (b, 0))
    in_hbm = pl.BlockSpec(memory_space=pltpu.HBM)
    return pl.pallas_call(
        functools.partial(_attn_sample_body, group=group, chunk=chunk),
        grid=(bsz,),
        in_specs=[shifted, in_hbm, in_hbm, local, local, shifted],
        out_specs=local,
        out_shape=jax.ShapeDtypeStruct((bsz * t_len, width), BF16),
        scratch_shapes=[pltpu.VMEM((2, n_heads, p_len, dh), F32),
                        pltpu.VMEM((2, n_heads, p_len, dh), F32),
                        pltpu.SemaphoreType.DMA((2, 2, n_heads))],
        compiler_params=_params(1),
        name="attn_sample",
    )(q, ck, cv, kn, vn, za)


def _conv_body(u_ref, g_ref, st_ref, cw_ref, cb_ref, y_ref, ns_ref):
    u = u_ref[...]
    st = st_ref[0]
    t_len = u.shape[0]
    row = lax.broadcasted_iota(jnp.int32, u.shape, 0)
    u1 = jnp.where(row == 0, st[1:2, :], pltpu.roll(u, 1, axis=0))
    u2 = jnp.where(row == 0, st[0:1, :],
                   jnp.where(row == 1, st[1:2, :], pltpu.roll(u, 2, axis=0)))
    conv = cb_ref[...] + u2 * cw_ref[0:1, :] + u1 * cw_ref[1:2, :] + u * cw_ref[2:3, :]
    y_ref[...] = (g_ref[...].astype(F32) * conv).astype(y_ref.dtype)
    ns_ref[0] = u_ref[t_len - (CONV_WIDTH - 1):, :]


def _conv(u, gate, state, conv_w, conv_b, *, row_offset, t_len, tc):
    bsz = state.shape[0]
    ch = u.shape[1]
    tc = min(tc, ch)
    r = row_offset // t_len
    in_spec = pl.BlockSpec((t_len, tc), lambda b, c: (r + b, c))
    st_spec = pl.BlockSpec((1, CONV_WIDTH - 1, tc), lambda b, c: (b, 0, c))
    return pl.pallas_call(
        _conv_body,
        grid=(bsz, ch // tc),
        in_specs=[in_spec, in_spec, st_spec,
                  pl.BlockSpec((CONV_WIDTH, tc), lambda b, c: (0, c)),
                  pl.BlockSpec((1, tc), lambda b, c: (0, c))],
        out_specs=[pl.BlockSpec((t_len, tc), lambda b, c: (b, c)), st_spec],
        out_shape=[jax.ShapeDtypeStruct((bsz * t_len, ch), BF16),
                   jax.ShapeDtypeStruct((bsz, CONV_WIDTH - 1, ch), F32)],
        compiler_params=_params(2),
        name="short_conv",
    )(u, gate, state, conv_w, conv_b)


def _merge_body(ya_ref, yb_ref, wa_ref, wb_ref, ga_ref, gb_ref, o_ref):
    p_a = jnp.dot(ya_ref[...], wa_ref[...], preferred_element_type=F32)
    p_b = jnp.dot(yb_ref[...], wb_ref[...], preferred_element_type=F32)
    merged = ga_ref[...].astype(F32) * p_a + gb_ref[...].astype(F32) * p_b
    o_ref[...] = merged.astype(o_ref.dtype)


def _merge(y_a, y_b, w_a, w_b, gates, *, row_offset, tm, tn):
    m, ka = y_a.shape
    kb = y_b.shape[1]
    d = w_a.shape[1]
    tn = min(tn, d)
    nj = d // tn
    r = row_offset // tm
    return pl.pallas_call(
        _merge_body,
        grid=(m // tm, nj),
        in_specs=[pl.BlockSpec((tm, ka), lambda i, j: (i, 0)),
                  pl.BlockSpec((tm, kb), lambda i, j: (i, 0)),
                  pl.BlockSpec((ka, tn), lambda i, j: (0, j)),
                  pl.BlockSpec((kb, tn), lambda i, j: (0, j)),
                  pl.BlockSpec((tm, tn), lambda i, j: (r + i, j)),
                  pl.BlockSpec((tm, tn), lambda i, j: (r + i, nj + j))],
        out_specs=pl.BlockSpec((tm, tn), lambda i, j: (i, j)),
        out_shape=jax.ShapeDtypeStruct((m, d), BF16),
        compiler_params=_params(2),
        name="gated_merge",
    )(y_a, y_b, w_a, w_b, gates, gates)


def _out_body(m_ref, w_ref, x_ref, g_ref, b_ref, o_ref, *, alpha):
    pre = alpha * x_ref[...] + jnp.dot(m_ref[...], w_ref[...], preferred_element_type=F32)
    mu = jnp.mean(pre, axis=1, keepdims=True)
    cen = pre - mu
    var = jnp.mean(cen * cen, axis=1, keepdims=True)
    o_ref[...] = cen * lax.rsqrt(var + LN_EPS) * g_ref[...] + b_ref[...]


def _out_proj_ln(merged, w_o, x, ln_g, ln_b, *, alpha, tm):
    m, d = x.shape
    tm = min(tm, m)
    row_spec = lambda: pl.BlockSpec((tm, d), lambda i: (i, 0))
    const_spec = lambda r: pl.BlockSpec((r, d), lambda i: (0, 0), pipeline_mode=pl.Buffered(1))
    return pl.pallas_call(
        functools.partial(_out_body, alpha=alpha),
        grid=(m // tm,),
        in_specs=[row_spec(), const_spec(d), row_spec(), const_spec(1), const_spec(1)],
        out_specs=row_spec(),
        out_shape=jax.ShapeDtypeStruct((m, d), F32),
        compiler_params=_params(1, vmem_limit_bytes=OUT_PROJ_VMEM_LIMIT_BYTES),
        name="out_proj_ln",
    )(merged, w_o, x, ln_g, ln_b)


def _layer(x, past_k, past_v, past_conv, w_in, b_in, conv_w, conv_b, w_a, w_b, w_o,
           ln_g, ln_b, rounded, *, n_heads, dh, alpha):
    bsz, t_len, d = x.shape
    m = bsz * t_len
    w_att = n_heads * dh
    w_conv = conv_w.shape[1]
    tm = min(ROW_TILE, m)
    x2d = x.reshape(m, d)
    q_scale = dh ** -0.5 * math.log2(math.e)

    o_q, o_k, o_v, o_za = 0, w_att, 2 * w_att, 3 * w_att
    o_bg = 4 * w_att
    o_cg, o_h, o_zb = o_bg + w_conv, o_bg + 2 * w_conv, o_bg + 3 * w_conv
    o_g = o_bg + 4 * w_conv
    ride = rounded is None
    xb = x2d.astype(BF16)
    conv_in = lambda cg, h: (cg * h,)
    if ride:
        cols_u = slice(o_cg, o_cg + 2 * w_conv)
        w_gates, u = _proj(xb, w_in[:, cols_u].astype(BF16), b_in[:, cols_u], [0, w_conv],
                           w_conv, conv_in, [F32], tm=tm, tn=512, name="proj_u",
                           round_cols=(w_in, o_g, 2 * d))
        w_main, gates = _proj(xb, w_gates, b_in[:, o_g:], [0], 2 * d, lambda a: (_sigmoid(a),),
                              [BF16], tm=tm, tn=1024, name="proj_gates",
                              round_cols=(w_in, 0, o_g))
    else:
        w_main, w_gates = rounded[0], rounded[4]
        (gates,) = _proj(xb, w_gates, b_in[:, o_g:], [0], 2 * d, lambda a: (_sigmoid(a),),
                         [BF16], tm=tm, tn=1024, name="proj_gates")

    proj = functools.partial(_proj, xb, w_main, b_in, tm=tm)
    if not ride:
        (u,) = proj([o_cg, o_h], w_conv, conv_in, [F32], tn=512, name="proj_u")
    q_out = proj([o_q], w_att, lambda a: (a * q_scale,), [BF16], tn=1024, name="proj_q",
                 round_cols=(w_a, 0, w_a.shape[1]) if ride else None)
    k, kb16 = _proj_heads(xb, w_main, b_in, o_k, w_att, dh, tm=tm, tn=1024, name="proj_k")
    v, vb16 = _proj_heads(xb, w_main, b_in, o_v, w_att, dh, tm=tm, tn=1024, name="proj_v")
    za_out = proj([o_za], w_att, lambda a: (_silu(a),), [BF16], tn=1024, name="proj_za",
                  round_cols=(w_o, 0, w_o.shape[1]) if ride else None)
    gate_b_out = proj([o_bg, o_zb], w_conv, lambda bg, zb: (_silu(zb) * bg,), [BF16], tn=512,
                      name="proj_gate_b", round_cols=(w_b, 0, w_b.shape[1]) if ride else None)
    q, za, gate_b = q_out[-1], za_out[-1], gate_b_out[-1]
    if ride:
        rounded = (w_main, q_out[0], gate_b_out[0], za_out[0], w_gates)
    _, w_a, w_b, w_o, _ = rounded

    if past_k is None:
        y_a = _attn_prompt(q, kb16, vb16, za, bsz=bsz, t_len=t_len, dh=dh, heads_per_step=4,
                           blk=ATT_BLOCK)
        past_conv = jnp.zeros((bsz, CONV_WIDTH - 1, w_conv), F32)
        tc = 512
    else:
        y_a = _attn_sample(q, past_k, past_v, kb16, vb16, za, row_offset=0, t_len=t_len,
                           group=4, chunk=ATT_BLOCK)
        tc = w_conv
    y_b, new_conv = _conv(u, gate_b, past_conv, conv_w, conv_b, row_offset=0, t_len=t_len, tc=tc)
    merged = _merge(y_a, y_b, w_a, w_b, gates, row_offset=0, tm=tm, tn=1024)
    y = _out_proj_ln(merged, w_o, x2d, ln_g, ln_b, alpha=alpha, tm=256)
    return (y.reshape(bsz, t_len, d), k.reshape(bsz, t_len, n_heads, dh),
            v.reshape(bsz, t_len, n_heads, dh), new_conv), rounded


def kernel(x_prompt, x_sample, cache_k, cache_v, state_conv, w_in, b_in, conv_w, conv_b,
           w_a, w_b, w_o, ln_g, ln_b):
    depth = w_in.shape[0]
    n_heads, dh = cache_k.shape[-2:]
    alpha = (2.0 * depth) ** 0.25
    xp, xs = x_prompt, x_sample
    kp, vp, cp, kn, vn, cn = [], [], [], [], [], []
    for l in range(depth):
        wts = (w_in[l], b_in[l][None, :], conv_w[l], conv_b[l][None, :], w_a[l], w_b[l], w_o[l],
               ln_g[l][None, :], ln_b[l][None, :])
        layer = functools.partial(_layer, n_heads=n_heads, dh=dh, alpha=alpha)
        (xp, k1, v1, c1), rounded = layer(xp, None, None, None, *wts, None)
        (xs, k2, v2, c2), _ = layer(xs, cache_k[l], cache_v[l], state_conv[l], *wts, rounded)
        kp.append(k1); vp.append(v1); cp.append(c1)
        kn.append(k2); vn.append(v2); cn.append(c2)
    return (xp, xs, jnp.stack(kp), jnp.stack(vp), jnp.stack(cp),
            jnp.stack(kn), jnp.stack(vn), jnp.stack(cn))
```
